```python
import jax, jax.numpy as jnp
from jax import lax
import numpy as np

D_MODEL = 4096
BATCH = 4
SEQ = 2048
DEPTH = 2
DEC_BATCH = 128
DEC_SEQ = 1
PAST_LEN = 16384
PAGE_SIZE = 128

N_EVEN = (DEPTH + 1) // 2
N_ODD = DEPTH // 2
D_A = D_MODEL // 2
D_B = D_MODEL // 2
POOL_WINDOWS = (2, 4, 8, 16)
N_POOL = len(POOL_WINDOWS)
D_BG = D_B // N_POOL
D_IN = 3 * D_A + D_B
CONV_A_K = 3
D_C = D_MODEL
CONV_C_K = 31
N_EXPERTS = 32
TOP_K = 4
D_EXPERT = D_MODEL // 2
SWIGLU_LIMIT = 7.0
SWIGLU_ALPHA = 1.702
MOE_BLOCK = 128
N_MOD = 6
RMS_EPS = 1e-6
LN_EPS = 1e-5
CONV_A_BUF = CONV_A_K - 1
POOL_BUF = max(POOL_WINDOWS) - 1
CONV_C_BUF = CONV_C_K - 1

kernel_name = 'hybrid_shortconv_pool_conformer_moe_step'


def rms_norm(x, g):
    xf = x.astype(jnp.float32)
    y = xf * lax.rsqrt(jnp.mean(xf * xf, axis=-1, keepdims=True) + RMS_EPS)
    return (y * g.astype(jnp.float32)).astype(x.dtype)


def layer_norm(x, g, b):
    xf = x.astype(jnp.float32)
    mu = jnp.mean(xf, axis=-1, keepdims=True)
    var = jnp.mean(jnp.square(xf - mu), axis=-1, keepdims=True)
    y = (xf - mu) * lax.rsqrt(var + LN_EPS) * g.astype(jnp.float32) + b.astype(jnp.float32)
    return y.astype(x.dtype)


def causal_dwconv(ext, w):
    return lax.conv_general_dilated(ext, w[:, None, :].astype(ext.dtype), window_strides=(1,), padding='VALID',
                                    dimension_numbers=('NWC', 'WIO', 'NWC'), feature_group_count=ext.shape[-1])


def causal_multiscale_pool(ext, pos0):
    n, tot, _ = ext.shape
    L = tot - POOL_BUF
    f = ext.astype(jnp.float32)
    cs = jnp.concatenate([jnp.zeros((n, 1, D_B), jnp.float32), jnp.cumsum(f, axis=1)], axis=1)
    i = jnp.arange(L)
    end = POOL_BUF + i + 1
    outs = []
    for g, w in enumerate(POOL_WINDOWS):
        sl = slice(g * D_BG, (g + 1) * D_BG)
        csg = cs[..., sl]
        win_sum = csg[:, end] - csg[:, jnp.maximum(end - w, 0)]
        count = jnp.minimum(pos0 + i + 1, w).astype(jnp.float32)
        outs.append(win_sum / count[None, :, None] - f[:, POOL_BUF:, sl])
    return jnp.stack(outs, axis=2)


def mixer_even(h, buf_a, buf_b, pos0, w_in, conv_w, pool_w, pool_scale, w_out):
    n, L, _ = h.shape
    proj = h @ w_in
    a_pre, a_post, a_val, b_in = jnp.split(proj, [D_A, 2 * D_A, 3 * D_A], axis=-1)
    ext_a = jnp.concatenate([buf_a.astype(h.dtype), a_pre * a_val], axis=1)
    y_a = a_post * causal_dwconv(ext_a, conv_w)
    ext_b = jnp.concatenate([buf_b.astype(h.dtype), b_in], axis=1)
    z = causal_multiscale_pool(ext_b, pos0).astype(h.dtype)
    y_b = jnp.einsum('nlgc,gcd->nlgd', z, pool_w).reshape(n, L, D_B) * pool_scale
    y = jnp.concatenate([y_a, y_b], axis=-1) @ w_out
    return y, ext_a[:, -CONV_A_BUF:], ext_b[:, -POOL_BUF:]


def mixer_odd(h, buf_c, pw1_w, pw1_b, dw_w, dw_b, ln_g, ln_b, pw2_w, pw2_b):
    u = h @ pw1_w + pw1_b
    val, gate = jnp.split(u, 2, axis=-1)
    v = val * jax.nn.sigmoid(gate)
    ext = jnp.concatenate([buf_c.astype(h.dtype), v], axis=1)
    z = causal_dwconv(ext, dw_w) + dw_b
    z = layer_norm(z, ln_g, ln_b)
    z = z * jax.nn.sigmoid(z)
    return z @ pw2_w + pw2_b, ext[:, -CONV_C_BUF:]


def moe(x, w_router, b_router, w_gate_up, b_gate_up, w_down, b_down):
    n, L, d = x.shape
    t = n * L
    xf = x.reshape(t, d)
    logits = xf.astype(jnp.float32) @ w_router.astype(jnp.float32) + b_router.astype(jnp.float32)
    top_vals, top_idx = lax.top_k(logits, TOP_K)
    gates = jax.nn.softmax(top_vals, axis=-1)
    n_assign = t * TOP_K
    flat_e = top_idx.reshape(-1)
    order = jnp.argsort(flat_e)
    sorted_e = flat_e[order]
    sorted_tok = order // TOP_K
    sorted_gate = gates.reshape(-1)[order]
    counts = jnp.bincount(flat_e, length=N_EXPERTS)
    padded = (counts + MOE_BLOCK - 1) // MOE_BLOCK * MOE_BLOCK
    start = jnp.cumsum(counts) - counts
    pstart = jnp.cumsum(padded) - padded
    dest = pstart[sorted_e] + jnp.arange(n_assign) - start[sorted_e]
    n_blocks = -(-(n_assign + N_EXPERTS * (MOE_BLOCK - 1)) // MOE_BLOCK)
    buf_tok = jnp.full((n_blocks * MOE_BLOCK,), t, jnp.int32).at[dest].set(sorted_tok.astype(jnp.int32))
    x_pad = jnp.concatenate([xf, jnp.zeros((1, d), xf.dtype)], axis=0)
    xb = x_pad[buf_tok].reshape(n_blocks, MOE_BLOCK, d)
    block_e = jnp.minimum(jnp.searchsorted(jnp.cumsum(padded), jnp.arange(n_blocks) * MOE_BLOCK, side='right'),
                          N_EXPERTS - 1)

    def expert_block(args):
        xbk, e = args
        hh = xbk @ w_gate_up[e] + b_gate_up[e]
        glu = jnp.minimum(hh[:, ::2], SWIGLU_LIMIT)
        lin = jnp.clip(hh[:, 1::2], -SWIGLU_LIMIT, SWIGLU_LIMIT)
        act = glu * jax.nn.sigmoid(SWIGLU_ALPHA * glu) * (lin + 1)
        return act @ w_down[e] + b_down[e]

    yb = lax.map(expert_block, (xb, block_e)).reshape(n_blocks * MOE_BLOCK, d)
    contrib = yb[dest].astype(jnp.float32) * sorted_gate[:, None]
    out = jnp.zeros((t, d), jnp.float32).at[sorted_tok].add(contrib)
    return out.astype(x.dtype).reshape(n, L, d)


def trunk(x, c, bufs_a, bufs_b, bufs_c, pos0, p):
    n = x.shape[0]
    new_a, new_b, new_c = [], [], []
    c_act = jax.nn.silu(c)
    for l in range(DEPTH):
        mod = (c_act @ p['w_ada'][l] + p['b_ada'][l]).reshape(n, N_MOD, 1, D_MODEL)
        shift_m, scale_m, gate_m, shift_f, scale_f, gate_f = [mod[:, k] for k in range(N_MOD)]
        ng = p['norm_gains'][l]
        h = rms_norm(x, ng[0]) * (1 + scale_m) + shift_m
        if l % 2 == 0:
            e = l // 2
            y, sa, sb = mixer_even(h, bufs_a[e], bufs_b[e], pos0, p['ab_w_in'][e], p['a_conv_w'][e],
                                   p['b_pool_w'][e], p['b_pool_scale'][e], p['ab_w_out'][e])
            new_a.append(sa)
            new_b.append(sb)
        else:
            o = l // 2
            y, s_c = mixer_odd(h, bufs_c[o], p['c_pw1_w'][o], p['c_pw1_b'][o], p['c_dw_w'][o], p['c_dw_b'][o],
                               p['c_ln_g'][o], p['c_ln_b'][o], p['c_pw2_w'][o], p['c_pw2_b'][o])
            new_c.append(s_c)
        x = x + gate_m * rms_norm(y, ng[1])
        h = rms_norm(x, ng[2]) * (1 + scale_f) + shift_f
        y = moe(h, p['moe_w_router'][l], p['moe_b_router'][l], p['moe_w_gate_up'][l], p['moe_b_gate_up'][l],
                p['moe_w_down'][l], p['moe_b_down'][l])
        x = x + gate_f * rms_norm(y, ng[3])
    return x, jnp.stack(new_a), jnp.stack(new_b), jnp.stack(new_c)


def setup_inputs(seed: int = 0) -> dict:
    key = jax.random.key(seed)
    ks = iter(jax.random.split(key, 40))

    def nrm(shape, scale):
        return jax.random.normal(next(ks), shape, jnp.float32) * scale

    D = D_MODEL
    return {
        'x_prompt': nrm((BATCH, SEQ, D), 1.0),
        'x_sample': nrm((DEC_BATCH, DEC_SEQ, D), 1.0),
        'c_prompt': nrm((BATCH, D), 1.0),
        'c_sample': nrm((DEC_BATCH, D), 1.0),
        'state_conv_a': nrm((N_EVEN, DEC_BATCH, CONV_A_BUF, D_A), 1.0),
        'state_pool_b': nrm((N_EVEN, DEC_BATCH, POOL_BUF, D_B), 1.0),
        'state_conv_c': nrm((N_ODD, DEC_BATCH, CONV_C_BUF, D_C), 1.0),
        'w_ada': nrm((DEPTH, D, N_MOD * D), 0.5 * D ** -0.5),
        'b_ada': nrm((DEPTH, N_MOD * D), 0.02),
        'norm_gains': 1.0 + nrm((DEPTH, 4, D), 0.05),
        'ab_w_in': nrm((N_EVEN, D, D_IN), D ** -0.5),
        'a_conv_w': nrm((N_EVEN, CONV_A_K, D_A), CONV_A_K ** -0.5),
        'b_pool_w': nrm((N_EVEN, N_POOL, D_BG, D_BG), D_BG ** -0.5),
        'b_pool_scale': 1.0 + nrm((N_EVEN, D_B), 0.1),
        'ab_w_out': nrm((N_EVEN, D_A + D_B, D), (D_A + D_B) ** -0.5),
        'c_pw1_w': nrm((N_ODD, D, 2 * D_C), D ** -0.5),
        'c_pw1_b': nrm((N_ODD, 2 * D_C), 0.02),
        'c_dw_w': nrm((N_ODD, CONV_C_K, D_C), CONV_C_K ** -0.5),
        'c_dw_b': nrm((N_ODD, D_C), 0.02),
        'c_ln_g': 1.0 + nrm((N_ODD, D_C), 0.05),
        'c_ln_b': nrm((N_ODD, D_C), 0.02),
        'c_pw2_w': nrm((N_ODD, D_C, D), D_C ** -0.5),
        'c_pw2_b': nrm((N_ODD, D), 0.02),
        'moe_w_router': nrm((DEPTH, D, N_EXPERTS), D ** -0.5),
        'moe_b_router': nrm((DEPTH, N_EXPERTS), 0.01),
        'moe_w_gate_up': nrm((DEPTH, N_EXPERTS, D, 2 * D_EXPERT), D ** -0.5),
        'moe_b_gate_up': nrm((DEPTH, N_EXPERTS, 2 * D_EXPERT), 0.02),
        'moe_w_down': nrm((DEPTH, N_EXPERTS, D_EXPERT, D), D_EXPERT ** -0.5),
        'moe_b_down': nrm((DEPTH, N_EXPERTS, D), 0.02),
    }


def reference(x_prompt, x_sample, c_prompt, c_sample, state_conv_a, state_pool_b, state_conv_c,
              w_ada, b_ada, norm_gains, ab_w_in, a_conv_w, b_pool_w, b_pool_scale, ab_w_out,
              c_pw1_w, c_pw1_b, c_dw_w, c_dw_b, c_ln_g, c_ln_b, c_pw2_w, c_pw2_b,
              moe_w_router, moe_b_router, moe_w_gate_up, moe_b_gate_up, moe_w_down, moe_b_down):
    p = dict(w_ada=w_ada, b_ada=b_ada, norm_gains=norm_gains, ab_w_in=ab_w_in, a_conv_w=a_conv_w,
             b_pool_w=b_pool_w, b_pool_scale=b_pool_scale, ab_w_out=ab_w_out,
             c_pw1_w=c_pw1_w, c_pw1_b=c_pw1_b, c_dw_w=c_dw_w, c_dw_b=c_dw_b, c_ln_g=c_ln_g, c_ln_b=c_ln_b,
             c_pw2_w=c_pw2_w, c_pw2_b=c_pw2_b, moe_w_router=moe_w_router, moe_b_router=moe_b_router,
             moe_w_gate_up=moe_w_gate_up, moe_b_gate_up=moe_b_gate_up, moe_w_down=moe_w_down,
             moe_b_down=moe_b_down)
    nb = x_prompt.shape[0]
    zero_a = jnp.zeros((N_EVEN, nb, CONV_A_BUF, D_A), x_prompt.dtype)
    zero_b = jnp.zeros((N_EVEN, nb, POOL_BUF, D_B), x_prompt.dtype)
    zero_c = jnp.zeros((N_ODD, nb, CONV_C_BUF, D_C), x_prompt.dtype)
    y_prompt, pa, pb, pc = trunk(x_prompt, c_prompt, zero_a, zero_b, zero_c, 0, p)
    y_sample, sa, sb, s_c = trunk(x_sample, c_sample, state_conv_a, state_pool_b, state_conv_c, PAST_LEN, p)
    return (y_prompt, y_sample, pa, pb, pc, sa, sb, s_c)
```

```python
import functools

import jax
import jax.numpy as jnp
from jax import lax
from jax.experimental import pallas as pl
from jax.experimental.pallas import tpu as pltpu

f32 = jnp.float32
bf16 = jnp.bfloat16
i32 = jnp.int32

LANE = 128
V7X_VMEM_LIMIT_BYTES = 56 * 1024 * 1024

POOL_WINDOWS = (2, 4, 8, 16)
POOL_BUF = max(POOL_WINDOWS) - 1
CONV_A_K = 3
CONV_C_K = 31
TOP_K = 4
SWIGLU_LIMIT = 7.0
SWIGLU_ALPHA = 1.702
N_MOD = 6
RMS_EPS = 1e-6
LN_EPS = 1e-5
PAST_LEN = 16384

ROW_TILE = 128
MM_ROW_TILE = 640
MM_COL_TILE = 512
HALO_A = 16
HALO_C = 32
MOE_CHUNK = 128
MOE_SUPER = 9 * MOE_CHUNK
MOE_C1 = 512
MOE_C2 = 512
GATHER_ROWS = 256


def _params(sem, vmem=V7X_VMEM_LIMIT_BYTES):
    return pltpu.CompilerParams(dimension_semantics=sem, vmem_limit_bytes=vmem)


def _sigmoid(x):
    return 1.0 / (1.0 + jnp.exp(-x))


def _ada_body(c_ref, w_ref, b_ref, o_ref):
    c = c_ref[...]
    ca = (c * _sigmoid(c)).astype(bf16)
    o_ref[...] = jnp.dot(ca, w_ref[...].astype(bf16), preferred_element_type=f32) + b_ref[...]


def _ada(c_all, w_ada, b_ada):
    depth, d, n = w_ada.shape
    rows = c_all.shape[0]
    tn = MM_COL_TILE
    return pl.pallas_call(
        _ada_body,
        grid=(depth, n // tn),
        in_specs=[
            pl.BlockSpec((rows, d), lambda l, j: (0, 0)),
            pl.BlockSpec((None, d, tn), lambda l, j: (l, 0, j)),
            pl.BlockSpec((None, 1, tn), lambda l, j: (l, 0, j)),
        ],
        out_specs=pl.BlockSpec((None, rows, tn), lambda l, j: (l, 0, j)),
        out_shape=jax.ShapeDtypeStruct((depth, rows, n), f32),
        compiler_params=_params(("arbitrary", "arbitrary")),
        name="ada",
    )(c_all, w_ada, b_ada.reshape(depth, 1, n))


def _rms(x, g):
    return x * lax.rsqrt(jnp.mean(x * x, axis=-1, keepdims=True) + RMS_EPS) * g


def _rowwise_body(*refs, n_prompt_tiles, has_resid, has_norm, has_router):
    refs = list(refs)
    x_ref = refs.pop(0)
    is_sample = pl.program_id(0) == n_prompt_tiles

    def mod(ref_s, ref_p):
        return jnp.where(is_sample, ref_s[...], ref_p[0])

    x = x_ref[...]
    if has_resid:
        y_ref, gy_ref, gate_s, gate_p = refs[:4]
        refs = refs[4:]
        x = x + mod(gate_s, gate_p) * _rms(y_ref[...], gy_ref[...])
    if has_norm:
        gh_ref, scale_s, scale_p, shift_s, shift_p = refs[:5]
        refs = refs[5:]
        h = _rms(x, gh_ref[...]) * (1.0 + mod(scale_s, scale_p)) + mod(shift_s, shift_p)
    if has_router:
        wr_ref, br_ref = refs[:2]
        refs = refs[2:]
    outs = refs
    if has_resid:
        outs.pop(0)[...] = x
    if has_norm:
        h_ref = outs.pop(0)
        h_ref[...] = h.astype(h_ref.dtype)
    if has_router:
        idx_ref, gate_ref = outs
        logits = jnp.dot(h, wr_ref[...], preferred_element_type=f32, precision=lax.Precision.HIGHEST) + br_ref[...]
        n_exp = logits.shape[-1]
        lane = lax.broadcasted_iota(i32, logits.shape, 1)
        vals, idxs = [], []
        for _ in range(TOP_K):
            m = jnp.max(logits, axis=-1, keepdims=True)
            idx = jnp.min(jnp.where(logits == m, lane, n_exp), axis=-1, keepdims=True)
            vals.append(m)
            idxs.append(idx)
            logits = jnp.where(lane == idx, -jnp.inf, logits)
        es = [jnp.exp(v - vals[0]) for v in vals]
        denom = es[0] + es[1] + es[2] + es[3]
        out_lane = lax.broadcasted_iota(i32, idx_ref.shape, 1)
        idx_out = jnp.zeros(idx_ref.shape, i32)
        gate_out = jnp.zeros(gate_ref.shape, f32)
        for k in range(TOP_K):
            idx_out = jnp.where(out_lane == k, idxs[k], idx_out)
            gate_out = jnp.where(out_lane == k, es[k] / denom, gate_out)
        idx_ref[...] = idx_out
        gate_ref[...] = gate_out


def _rowwise(x, mod_s, mod_p, *, n_prompt_rows, rows_per_seq, resid=None, norm=None, router=None, h_dtype=bf16):
    r, d = x.shape
    tm = ROW_TILE
    n_prompt_tiles = n_prompt_rows // tm
    tiles_per_seq = rows_per_seq // tm
    n_batch = mod_p.shape[0]
    row_spec = pl.BlockSpec((tm, d), lambda i: (i, 0))
    vec_spec = pl.BlockSpec((1, d), lambda i: (0, 0))

    def mod_specs(col):
        return [
            pl.BlockSpec((mod_s.shape[0], d), lambda i: (0, col)),
            pl.BlockSpec((1, 1, d), lambda i: (jnp.minimum(i // tiles_per_seq, n_batch - 1), 0, col)),
        ]

    args, in_specs, out_shape, out_specs = [x], [row_spec], [], []
    if resid is not None:
        y, gain_y, gate_col = resid
        args += [y, gain_y.reshape(1, d), mod_s, mod_p]
        in_specs += [row_spec, vec_spec] + mod_specs(gate_col)
        out_shape.append(jax.ShapeDtypeStruct((r, d), f32))
        out_specs.append(row_spec)
    if norm is not None:
        gain_h, scale_col, shift_col = norm
        args += [gain_h.reshape(1, d), mod_s, mod_p, mod_s, mod_p]
        in_specs += [vec_spec] + mod_specs(scale_col) + mod_specs(shift_col)
        out_shape.append(jax.ShapeDtypeStruct((r, d), h_dtype))
        out_specs.append(row_spec)
    if router is not None:
        w_router, b_router = router
        n_exp = w_router.shape[1]
        args += [w_router, b_router.reshape(1, n_exp)]
        in_specs += [pl.BlockSpec((d, n_exp), lambda i: (0, 0)), pl.BlockSpec((1, n_exp), lambda i: (0, 0))]
        out_shape += [jax.ShapeDtypeStruct((r, LANE), i32), jax.ShapeDtypeStruct((r, LANE), f32)]
        out_specs += [pl.BlockSpec((tm, LANE), lambda i: (i, 0))] * 2
    body = functools.partial(_rowwise_body, n_prompt_tiles=n_prompt_tiles, has_resid=resid is not None,
                             has_norm=norm is not None, has_router=router is not None)
    return pl.pallas_call(
        body, grid=(r // tm,), in_specs=in_specs, out_specs=out_specs, out_shape=out_shape,
        compiler_params=_params(("arbitrary",)), name="rowwise",
    )(*args)


def _mm_body(*refs, has_bias):
    if has_bias:
        x_ref, w_ref, b_ref, o_ref, wb_ref = refs
    else:
        x_ref, w_ref, o_ref, wb_ref = refs

    @pl.when(pl.program_id(1) == 0)
    def _():
        wb_ref[...] = w_ref[...].astype(bf16)

    acc = jnp.dot(x_ref[...], wb_ref[...], preferred_element_type=f32)
    if has_bias:
        acc = acc + b_ref[...]
    o_ref[...] = acc


def _mm(x, w, bias=None):
    m, k = x.shape
    n = w.shape[1]
    tm, tn = MM_ROW_TILE, MM_COL_TILE
    args = [x, w]
    in_specs = [pl.BlockSpec((tm, k), lambda j, i: (i, 0)), pl.BlockSpec((k, tn), lambda j, i: (0, j))]
    if bias is not None:
        args.append(bias.reshape(1, n))
        in_specs.append(pl.BlockSpec((1, tn), lambda j, i: (0, j)))
    return pl.pallas_call(
        functools.partial(_mm_body, has_bias=bias is not None),
        grid=(n // tn, m // tm),
        in_specs=in_specs,
        out_specs=pl.BlockSpec((tm, tn), lambda j, i: (i, j)),
        out_shape=jax.ShapeDtypeStruct((m, n), f32),
        scratch_shapes=[pltpu.VMEM((k, tn), bf16)],
        compiler_params=_params(("arbitrary", "arbitrary")),
        name="mm",
    )(*args)


def _pool_mix(z, pw_ref, ps_ref, gi, c0, cw):
    yb = jnp.dot(z.astype(bf16), pw_ref[gi].astype(bf16), preferred_element_type=f32)
    return yb * ps_ref[:, c0:c0 + cw]


def _even_prompt_body(p_ref, ph_ref, ysm_ref, cw_ref, pw_ref, ps_ref, y_ref, sa_ref, sb_ref, eg_ref, eb_ref,
                      *, tl, da, tiles, n_tiles):
    step = pl.program_id(0)

    @pl.when(step == n_tiles)
    def _sample_rows():
        y_ref[...] = ysm_ref[...]

    @pl.when(step < n_tiles)
    def _prompt_rows():
        _even_prompt_tile(p_ref, ph_ref, cw_ref, pw_ref, ps_ref, y_ref, sa_ref, sb_ref, eg_ref, eb_ref,
                          tl=tl, da=da, tile_in_seq=step % tiles)


def _even_prompt_tile(p_ref, ph_ref, cw_ref, pw_ref, ps_ref, y_ref, sa_ref, sb_ref, eg_ref, eb_ref,
                      *, tl, da, tile_in_seq):
    first = tile_in_seq == 0
    a_pre, a_post = p_ref[:, 0:da], p_ref[:, da:2 * da]
    a_val, b_in = p_ref[:, 2 * da:3 * da], p_ref[:, 3 * da:4 * da]
    g = a_pre * a_val
    eg_ref[0:HALO_A, :] = jnp.where(first, 0.0, ph_ref[:, 0:da] * ph_ref[:, 2 * da:3 * da])
    eg_ref[HALO_A:, :] = g
    eb_ref[0:HALO_A, :] = jnp.where(first, 0.0, ph_ref[:, 3 * da:4 * da])
    eb_ref[HALO_A:, :] = b_in
    conv = (cw_ref[0:1, :] * eg_ref[HALO_A - 2:HALO_A - 2 + tl, :]
            + cw_ref[1:2, :] * eg_ref[HALO_A - 1:HALO_A - 1 + tl, :]
            + cw_ref[2:3, :] * g)
    y_ref[:, 0:da] = (a_post * conv).astype(y_ref.dtype)
    sa_ref[...] = eg_ref[HALO_A + tl - (CONV_A_K - 1):HALO_A + tl, :]
    sb_ref[...] = eb_ref[HALO_A + tl - POOL_BUF:HALO_A + tl, :]
    pos = tile_in_seq * tl + lax.broadcasted_iota(i32, (tl, 1), 0)
    cw = da // len(POOL_WINDOWS)
    for gi, w in enumerate(POOL_WINDOWS):
        c0 = gi * cw
        cur = b_in[:, c0:c0 + cw]
        s = cur
        for back in range(1, w):
            s = s + eb_ref[HALO_A - back:HALO_A - back + tl, c0:c0 + cw]
        count = jnp.minimum(pos + 1, w).astype(f32)
        z = s / count - cur
        y_ref[:, da + c0:da + c0 + cw] = _pool_mix(z, pw_ref, ps_ref, gi, c0, cw).astype(y_ref.dtype)


def _even_prompt(proj, y_sample, conv_w, pool_w, pool_scale, *, n_batch, seq):
    d_in = proj.shape[1]
    da = d_in // 4
    tl = ROW_TILE
    tiles = seq // tl
    n_tiles = n_batch * tiles
    halo_per_tile = tl // HALO_A
    ng, cg = pool_w.shape[0], pool_w.shape[1]
    assert y_sample.shape == (tl, 2 * da)

    def tile(s):
        return jnp.minimum(s, n_tiles - 1)

    def batch(s):
        return tile(s) // tiles

    return pl.pallas_call(
        functools.partial(_even_prompt_body, tl=tl, da=da, tiles=tiles, n_tiles=n_tiles),
        grid=(n_tiles + 1,),
        in_specs=[
            pl.BlockSpec((tl, d_in), lambda s: (tile(s), 0)),
            pl.BlockSpec((HALO_A, d_in), lambda s: (jnp.maximum(tile(s) * halo_per_tile - 1, 0), 0)),
            pl.BlockSpec((tl, 2 * da), lambda s: (0, 0)),
            pl.BlockSpec((CONV_A_K, da), lambda s: (0, 0)),
            pl.BlockSpec((ng, cg, cg), lambda s: (0, 0, 0)),
            pl.BlockSpec((1, da), lambda s: (0, 0)),
        ],
        out_specs=[
            pl.BlockSpec((tl, 2 * da), lambda s: (s, 0)),
            pl.BlockSpec((None, CONV_A_K - 1, da), lambda s: (batch(s), 0, 0)),
            pl.BlockSpec((None, POOL_BUF, da), lambda s: (batch(s), 0, 0)),
        ],
        out_shape=[
            jax.ShapeDtypeStruct(((n_tiles + 1) * tl, 2 * da), bf16),
            jax.ShapeDtypeStruct((n_batch, CONV_A_K - 1, da), f32),
            jax.ShapeDtypeStruct((n_batch, POOL_BUF, da), f32),
        ],
        scratch_shapes=[pltpu.VMEM((HALO_A + tl, da), f32), pltpu.VMEM((HALO_A + tl, da), f32)],
        compiler_params=_params(("arbitrary",)),
        name="even_prompt",
    )(proj, proj, y_sample, conv_w, pool_w, pool_scale.reshape(1, da))


def _even_sample_body(apre_ref, apost_ref, aval_ref, bin_ref, sa_ref, sb_ref, cw_ref, pw_ref, ps_ref,
                      y_ref, g_ref, *, ng):
    s_id = pl.program_id(0)

    @pl.when(s_id < ng)
    def _conv():
        g = apre_ref[...] * aval_ref[...]
        g_ref[...] = g
        conv = cw_ref[0:1, :] * sa_ref[0] + cw_ref[1:2, :] * sa_ref[1] + cw_ref[2:3, :] * g
        y_ref[...] = (apost_ref[...] * conv).astype(y_ref.dtype)

    @pl.when(s_id >= ng)
    def _pool():
        cur = bin_ref[...]
        s = cur
        z = jnp.zeros_like(cur)
        back = 1
        for gi, w in enumerate(POOL_WINDOWS):
            while back < w:
                s = s + sb_ref[POOL_BUF - back]
                back += 1
            count = float(min(PAST_LEN + 1, w))
            z = jnp.where(s_id - ng == gi, s / count - cur, z)
        yb = jnp.dot(z.astype(bf16), pw_ref[...].astype(bf16), preferred_element_type=f32) * ps_ref[...]
        y_ref[...] = yb.astype(y_ref.dtype)


def _even_sample(proj, state_a_t, state_b_t, conv_w, pool_w, pool_scale, *, row_block):
    da = proj.shape[1] // 4
    ng, cg = pool_w.shape[0], pool_w.shape[1]
    nb = state_a_t.shape[1]
    assert nb == ROW_TILE and cg * ng == da and ng == len(POOL_WINDOWS)

    def conv_chunk(s):
        return jnp.minimum(s, ng - 1)

    def pool_group(s):
        return jnp.maximum(s - ng, 0)

    def col(k):
        return pl.BlockSpec((nb, cg), lambda s: (row_block, k * ng + conv_chunk(s)))

    return pl.pallas_call(
        functools.partial(_even_sample_body, ng=ng),
        grid=(2 * ng,),
        in_specs=[
            col(0), col(1), col(2),
            pl.BlockSpec((nb, cg), lambda s: (row_block, 3 * ng + pool_group(s))),
            pl.BlockSpec((CONV_A_K - 1, nb, cg), lambda s: (0, 0, conv_chunk(s))),
            pl.BlockSpec((POOL_BUF, nb, cg), lambda s: (0, 0, pool_group(s))),
            pl.BlockSpec((CONV_A_K, cg), lambda s: (0, conv_chunk(s))),
            pl.BlockSpec((None, cg, cg), lambda s: (pool_group(s), 0, 0)),
            pl.BlockSpec((1, cg), lambda s: (0, pool_group(s))),
        ],
        out_specs=[
            pl.BlockSpec((nb, cg), lambda s: (0, s)),
            pl.BlockSpec((nb, cg), lambda s: (0, conv_chunk(s))),
        ],
        out_shape=[
            jax.ShapeDtypeStruct((nb, 2 * da), bf16),
            jax.ShapeDtypeStruct((nb, da), f32),
        ],
        compiler_params=_params(("arbitrary",)),
        name="even_sample",
    )(proj, proj, proj, proj, state_a_t, state_b_t, conv_w, pool_w, pool_scale.reshape(1, da))


def _ln_swish(z, g, b):
    mu = jnp.mean(z, axis=-1, keepdims=True)
    zc = z - mu
    var = jnp.mean(zc * zc, axis=-1, keepdims=True)
    y = zc * lax.rsqrt(var + LN_EPS) * g + b
    return y * _sigmoid(y)


def _odd_prompt_body(u_ref, uh_ref, zsm_ref, dw_ref, db_ref, lg_ref, lb_ref, z_ref, sc_ref, ext_ref,
                     *, tl, dc, tiles, n_tiles):
    step = pl.program_id(0)

    @pl.when(step == n_tiles)
    def _sample_rows():
        z_ref[...] = _ln_swish(zsm_ref[...], lg_ref[...], lb_ref[...]).astype(z_ref.dtype)

    @pl.when(step < n_tiles)
    def _prompt_rows():
        _odd_prompt_tile(u_ref, uh_ref, dw_ref, db_ref, lg_ref, lb_ref, z_ref, sc_ref, ext_ref,
                         tl=tl, dc=dc, first=step % tiles == 0)


def _odd_prompt_tile(u_ref, uh_ref, dw_ref, db_ref, lg_ref, lb_ref, z_ref, sc_ref, ext_ref, *, tl, dc, first):
    v = u_ref[:, 0:dc] * _sigmoid(u_ref[:, dc:2 * dc])
    vh = uh_ref[:, 0:dc] * _sigmoid(uh_ref[:, dc:2 * dc])
    ext_ref[0:HALO_C, :] = jnp.where(first, 0.0, vh)
    ext_ref[HALO_C:, :] = v
    off = HALO_C - (CONV_C_K - 1)
    acc = dw_ref[CONV_C_K - 1:CONV_C_K, :] * v + db_ref[...]
    for k in range(CONV_C_K - 1):
        acc = acc + dw_ref[k:k + 1, :] * ext_ref[off + k:off + k + tl, :]
    z_ref[...] = _ln_swish(acc, lg_ref[...], lb_ref[...]).astype(z_ref.dtype)
    sc_ref[...] = ext_ref[HALO_C + tl - (CONV_C_K - 1):HALO_C + tl, :]


def _odd_prompt(u, conv_sample, dw_w, dw_b, ln_g, ln_b, *, n_batch, seq):
    dc = u.shape[1] // 2
    tl = ROW_TILE
    tiles = seq // tl
    n_tiles = n_batch * tiles
    halo_per_tile = tl // HALO_C
    assert conv_sample.shape == (tl, dc)
    vec = pl.BlockSpec((1, dc), lambda s: (0, 0))

    def tile(s):
        return jnp.minimum(s, n_tiles - 1)

    return pl.pallas_call(
        functools.partial(_odd_prompt_body, tl=tl, dc=dc, tiles=tiles, n_tiles=n_tiles),
        grid=(n_tiles + 1,),
        in_specs=[
            pl.BlockSpec((tl, 2 * dc), lambda s: (tile(s), 0)),
            pl.BlockSpec((HALO_C, 2 * dc), lambda s: (jnp.maximum(tile(s) * halo_per_tile - 1, 0), 0)),
            pl.BlockSpec((tl, dc), lambda s: (0, 0)),
            pl.BlockSpec((CONV_C_K, dc), lambda s: (0, 0)),
            vec, vec, vec,
        ],
        out_specs=[
            pl.BlockSpec((tl, dc), lambda s: (s, 0)),
            pl.BlockSpec((None, CONV_C_K - 1, dc), lambda s: (tile(s) // tiles, 0, 0)),
        ],
        out_shape=[
            jax.ShapeDtypeStruct(((n_tiles + 1) * tl, dc), bf16),
            jax.ShapeDtypeStruct((n_batch, CONV_C_K - 1, dc), f32),
        ],
        scratch_shapes=[pltpu.VMEM((HALO_C + tl, dc), f32)],
        compiler_params=_params(("arbitrary",)),
        name="odd_prompt",
    )(u, u, conv_sample, dw_w, dw_b.reshape(1, dc), ln_g.reshape(1, dc), ln_b.reshape(1, dc))


def _odd_sample_conv_body(val_ref, gate_ref, sc_ref, dw_ref, db_ref, v_ref, a_ref):
    v = val_ref[...] * _sigmoid(gate_ref[...])
    v_ref[...] = v
    acc = dw_ref[CONV_C_K - 1:CONV_C_K, :] * v + db_ref[...]
    for k in range(CONV_C_K - 1):
        acc = acc + dw_ref[k:k + 1, :] * sc_ref[k]
    a_ref[...] = acc


def _odd_sample(u, state_c_t, dw_w, dw_b, *, row_block):
    dc = u.shape[1] // 2
    nb = state_c_t.shape[1]
    cc = MM_COL_TILE
    nchunk = dc // cc
    return pl.pallas_call(
        _odd_sample_conv_body,
        grid=(nchunk,),
        in_specs=[
            pl.BlockSpec((nb, cc), lambda j: (row_block, j)),
            pl.BlockSpec((nb, cc), lambda j: (row_block, nchunk + j)),
            pl.BlockSpec((CONV_C_K - 1, nb, cc), lambda j: (0, 0, j)),
            pl.BlockSpec((CONV_C_K, cc), lambda j: (0, j)),
            pl.BlockSpec((1, cc), lambda j: (0, j)),
        ],
        out_specs=[pl.BlockSpec((nb, cc), lambda j: (0, j))] * 2,
        out_shape=[jax.ShapeDtypeStruct((nb, dc), f32)] * 2,
        compiler_params=_params(("arbitrary",)),
        name="odd_sample_conv",
    )(u, u, state_c_t, dw_w, dw_b.reshape(1, dc))


def _gather_body(src_ref, h_hbm, o_ref, buf_ref, sem):
    base = pl.program_id(0) * GATHER_ROWS

    def row_copy(r, tok):
        return pltpu.make_async_copy(h_hbm.at[pl.ds(tok, 1), :], buf_ref.at[pl.ds(r, 1), :], sem)

    def start(r, carry):
        row_copy(r, src_ref[base + r]).start()
        return carry

    def wait(r, carry):
        row_copy(r, src_ref[base + r]).wait()
        return carry

    lax.fori_loop(0, GATHER_ROWS, start, 0, unroll=8)
    lax.fori_loop(0, GATHER_ROWS, wait, 0, unroll=8)
    o_ref[...] = buf_ref[...].astype(o_ref.dtype)


def _moe_gather(h, src_tok):
    p_total = src_tok.shape[0]
    d = h.shape[1]
    return pl.pallas_call(
        _gather_body,
        grid_spec=pltpu.PrefetchScalarGridSpec(
            num_scalar_prefetch=1,
            grid=(p_total // GATHER_ROWS,),
            in_specs=[pl.BlockSpec(memory_space=pl.ANY)],
            out_specs=pl.BlockSpec((GATHER_ROWS, d), lambda i, src: (i, 0)),
            scratch_shapes=[pltpu.VMEM((GATHER_ROWS, d), f32), pltpu.SemaphoreType.DMA(())],
        ),
        out_shape=jax.ShapeDtypeStruct((p_total, d), bf16),
        compiler_params=_params(("arbitrary",)),
        name="moe_gather",
    )(src_tok, h)


def _pair_act(h):
    n = h.shape[1]
    glu = jnp.minimum(h, SWIGLU_LIMIT)
    lin = jnp.clip(h, -SWIGLU_LIMIT, SWIGLU_LIMIT) + 1.0
    lin_next = pltpu.roll(lin, n - 1, axis=1)
    return glu * _sigmoid(SWIGLU_ALPHA * glu) * lin_next


def _expert_body(se_ref, sblk_ref, snc_ref, tail_ref, x_ref, w1_ref, b1_ref, wd_ref, bd_ref, ys_hbm,
                 w1b_ref, wdb_ref, act_ref, perm_ref, obuf_ref, osem, *, j1, k2):
    del se_ref
    i, j = pl.program_id(0), pl.program_id(1)
    n_chunks = snc_ref[i]
    half = MOE_C1 // 2
    ch = MOE_CHUNK
    row0 = sblk_ref[i] * ch

    def out_copy(src_slot, src_chunk, dst_row, col_tile):
        return pltpu.make_async_copy(
            obuf_ref.at[src_slot, pl.ds(pl.multiple_of(src_chunk * ch, ch), ch), :],
            ys_hbm.at[pl.ds(pl.multiple_of(dst_row, ch), ch), pl.ds(pl.multiple_of(col_tile * MOE_C2, MOE_C2), MOE_C2)],
            osem.at[src_slot])

    def wait_tile(slot, col_tile):
        def body(c, carry):
            out_copy(slot, c, row0 + c * ch, col_tile).wait()
            return carry
        lax.fori_loop(0, n_chunks, body, 0)

    @pl.when(j < j1)
    def _gate_up():
        w1b_ref[...] = w1_ref[...].astype(bf16)
        even = (lax.broadcasted_iota(i32, (ch, half), 1) & 1) == 0

        def chunk(c, carry):
            r0 = pl.multiple_of(c * ch, ch)
            hh = jnp.dot(x_ref[pl.ds(r0, ch), :], w1b_ref[...], preferred_element_type=f32) + b1_ref[...]
            pa = _pair_act(hh[:, :half])
            pb = _pair_act(hh[:, half:])
            act_ref[j, pl.ds(r0, ch), :] = jnp.where(even, pa, pltpu.roll(pb, 1, axis=1)).astype(bf16)
            return carry

        lax.fori_loop(0, n_chunks, chunk, 0)

    @pl.when(j >= j1)
    def _down():
        k = j - j1
        slot = lax.rem(k, 2)

        @pl.when(k >= 2)
        def _():
            wait_tile(slot, k - 2)

        for jj in range(j1):
            for c in range(MOE_C2 // LANE):
                rows = slice(jj * half, (jj + 1) * half)
                cols = slice(c * LANE, (c + 1) * LANE)
                perm_ref[c, pl.ds(0, half // 2, stride=2), :] = wd_ref[jj * half:jj * half + half // 2, cols]
                perm_ref[c, pl.ds(1, half // 2, stride=2), :] = wd_ref[jj * half + half // 2:(jj + 1) * half, cols]
                wdb_ref[rows, cols] = perm_ref[c].astype(bf16)

        def chunk(c, carry):
            r0 = pl.multiple_of(c * ch, ch)
            a = jnp.concatenate([act_ref[jj, pl.ds(r0, ch), :] for jj in range(j1)], axis=1)
            obuf_ref[slot, pl.ds(r0, ch), :] = jnp.dot(a, wdb_ref[...], preferred_element_type=f32) + bd_ref[...]
            out_copy(slot, c, row0 + c * ch, k).start()
            return carry

        lax.fori_loop(0, n_chunks, chunk, 0)

        @pl.when(k == k2 - 1)
        def _drain():
            wait_tile(1 - slot, k - 1)
            wait_tile(slot, k)

            @pl.when(i == pl.num_programs(0) - 1)
            def _tail():
                obuf_ref[0, 0:ch, :] = jnp.zeros((ch, MOE_C2), f32)
                tail_row0, n_tail = tail_ref[0] * ch, tail_ref[1]

                def start(t, carry):
                    for kk in range(k2):
                        out_copy(0, 0, tail_row0 + t * ch, kk).start()
                    return carry

                def wait(t, carry):
                    for kk in range(k2):
                        out_copy(0, 0, tail_row0 + t * ch, kk).wait()
                    return carry

                lax.fori_loop(0, n_tail, start, 0)
                lax.fori_loop(0, n_tail, wait, 0)


def _moe_experts(xs, w_gate_up, b_gate_up, w_down, b_down, layer, ys_rows, n_super, sup_e, sup_blk, sup_nc, tail):
    d = xs.shape[1]
    de = w_down.shape[2]
    j1 = (2 * de) // MOE_C1
    k2 = d // MOE_C2
    assert k2 >= 2
    s = MOE_SUPER
    el = pl.Element

    def w1_idx(i, j, se, sb, nc, tl):
        return (layer, se[i], 0, jnp.minimum(j, j1 - 1))

    def wd_idx(i, j, se, sb, nc, tl):
        return (layer, se[i], 0, jnp.maximum(j - j1, 0))

    return pl.pallas_call(
        functools.partial(_expert_body, j1=j1, k2=k2),
        grid_spec=pltpu.PrefetchScalarGridSpec(
            num_scalar_prefetch=4,
            grid=(n_super, j1 + k2),
            in_specs=[
                pl.BlockSpec((el(s), el(d)), lambda i, j, se, sb, nc, tl: (sb[i] * MOE_CHUNK, 0),
                             pipeline_mode=pl.Buffered(1)),
                pl.BlockSpec((None, None, d, MOE_C1), w1_idx),
                pl.BlockSpec((None, None, 1, MOE_C1), w1_idx),
                pl.BlockSpec((None, None, de, MOE_C2), wd_idx),
                pl.BlockSpec((None, None, 1, MOE_C2), wd_idx),
            ],
            out_specs=pl.BlockSpec(memory_space=pl.ANY),
            scratch_shapes=[
                pltpu.VMEM((d, MOE_C1), bf16),
                pltpu.VMEM((de, MOE_C2), bf16),
                pltpu.VMEM((j1, s, MOE_C1 // 2), bf16),
                pltpu.VMEM((MOE_C2 // LANE, MOE_C1 // 2, LANE), f32),
                pltpu.VMEM((2, s, MOE_C2), f32),
                pltpu.SemaphoreType.DMA((2,)),
            ],
        ),
        out_shape=jax.ShapeDtypeStruct((ys_rows, d), f32),
        compiler_params=_params(("arbitrary", "arbitrary")),
        name="moe_experts",
    )(sup_e, sup_blk, sup_nc, tail, xs, w_gate_up, b_gate_up.reshape(*b_gate_up.shape[:2], 1, -1),
      w_down, b_down.reshape(*b_down.shape[:2], 1, -1))


def _combine_body(dest_ref, ys_hbm, gate_ref, o_ref, buf_ref, sem):
    tm = o_ref.shape[0]
    base = pl.program_id(0) * tm

    def row_copy(r, k):
        p = dest_ref[(base + r) * TOP_K + k]
        return pltpu.make_async_copy(ys_hbm.at[pl.ds(p, 1), :], buf_ref.at[k, pl.ds(r, 1), :], sem)

    def start(r, carry):
        for k in range(TOP_K):
            row_copy(r, k).start()
        return carry

    def wait(r, carry):
        for k in range(TOP_K):
            row_copy(r, k).wait()
        return carry

    lax.fori_loop(0, tm, start, 0, unroll=4)
    lax.fori_loop(0, tm, wait, 0, unroll=4)
    g = gate_ref[...]
    acc = buf_ref[0] * g[:, 0:1]
    for k in range(1, TOP_K):
        acc = acc + buf_ref[k] * g[:, k:k + 1]
    o_ref[...] = acc


def _moe_combine(ys, dest, gates, n_rows):
    d = ys.shape[1]
    tm = ROW_TILE
    return pl.pallas_call(
        _combine_body,
        grid_spec=pltpu.PrefetchScalarGridSpec(
            num_scalar_prefetch=1,
            grid=(n_rows // tm,),
            in_specs=[pl.BlockSpec(memory_space=pl.ANY), pl.BlockSpec((tm, LANE), lambda i, dest: (i, 0))],
            out_specs=pl.BlockSpec((tm, d), lambda i, dest: (i, 0)),
            scratch_shapes=[pltpu.VMEM((TOP_K, tm, d), f32), pltpu.SemaphoreType.DMA(())],
        ),
        out_shape=jax.ShapeDtypeStruct((n_rows, d), f32),
        compiler_params=_params(("arbitrary",)),
        name="moe_combine",
    )(dest, ys, gates)


def _moe_plan(top_idx, n_exp):
    t = top_idx.shape[0]
    n_assign = t * TOP_K
    flat_e = top_idx.reshape(-1)
    order = jnp.argsort(flat_e)
    sorted_e = flat_e[order]
    counts = jnp.bincount(flat_e, length=n_exp).astype(i32)
    padded = (counts + MOE_CHUNK - 1) // MOE_CHUNK * MOE_CHUNK
    start = jnp.cumsum(counts) - counts
    pstart = jnp.cumsum(padded) - padded
    dest_sorted = (pstart[sorted_e] + jnp.arange(n_assign, dtype=i32) - start[sorted_e]).astype(i32)
    dest = jnp.zeros((n_assign,), i32).at[order].set(dest_sorted)
    max_rows = -(-(n_assign + n_exp * (MOE_CHUNK - 1)) // MOE_CHUNK) * MOE_CHUNK
    p_total = -(-(max_rows + MOE_SUPER) // GATHER_ROWS) * GATHER_ROWS
    src_tok = jnp.zeros((p_total,), i32).at[dest_sorted].set((order // TOP_K).astype(i32))
    n_sup_e = (padded + MOE_SUPER - 1) // MOE_SUPER
    cum = jnp.cumsum(n_sup_e)
    max_super = n_exp + max_rows // MOE_SUPER
    sid = jnp.arange(max_super, dtype=i32)
    sup_e = jnp.minimum(jnp.searchsorted(cum, sid, side="right"), n_exp - 1).astype(i32)
    within = sid - (cum[sup_e] - n_sup_e[sup_e])
    row0 = pstart[sup_e] + within * MOE_SUPER
    rows = jnp.clip(padded[sup_e] - within * MOE_SUPER, 0, MOE_SUPER)
    rows = jnp.where(sid < cum[-1], rows, 0)
    used = jnp.sum(padded)
    tail = jnp.stack([used // MOE_CHUNK, (max_rows - used) // MOE_CHUNK]).astype(i32)
    return dict(dest=dest, src_tok=src_tok, n_super=cum[-1].astype(i32), sup_e=sup_e, ys_rows=max_rows,
                sup_blk=(row0 // MOE_CHUNK).astype(i32), sup_nc=(rows // MOE_CHUNK).astype(i32), tail=tail)


def _moe(h, top_idx, gates, w_gate_up, b_gate_up, w_down, b_down, layer):
    plan = _moe_plan(top_idx[:, :TOP_K], w_down.shape[1])
    xs = _moe_gather(h, plan["src_tok"])
    ys = _moe_experts(xs, w_gate_up, b_gate_up, w_down, b_down, layer, plan["ys_rows"],
                      plan["n_super"], plan["sup_e"], plan["sup_blk"], plan["sup_nc"], plan["tail"])
    return _moe_combine(ys, plan["dest"], gates, h.shape[0])


def kernel(x_prompt, x_sample, c_prompt, c_sample, state_conv_a, state_pool_b, state_conv_c, w_ada, b_ada, norm_gains, ab_w_in, a_conv_w, b_pool_w, b_pool_scale, ab_w_out, c_pw1_w, c_pw1_b, c_dw_w, c_dw_b, c_ln_g, c_ln_b, c_pw2_w, c_pw2_b, moe_w_router, moe_b_router, moe_w_gate_up, moe_b_gate_up, moe_w_down, moe_b_down):
    n_batch, seq, d = x_prompt.shape
    n_dec, dec_seq, _ = x_sample.shape
    depth = w_ada.shape[0]
    assert dec_seq == 1 and n_dec == ROW_TILE and seq % ROW_TILE == 0
    n_prompt_rows = n_batch * seq
    r = n_prompt_rows + n_dec
    assert r % MM_ROW_TILE == 0
    sample_block = n_prompt_rows // ROW_TILE

    x = jnp.concatenate([x_prompt.reshape(n_prompt_rows, d), x_sample.reshape(n_dec, d)], axis=0)
    pad = (-(n_dec + n_batch)) % 8
    c_all = jnp.concatenate([c_sample, c_prompt, jnp.zeros((pad, d), f32)], axis=0)
    mod_all = _ada(c_all, w_ada, b_ada)
    rowwise = functools.partial(_rowwise, n_prompt_rows=n_prompt_rows, rows_per_seq=seq)

    new_a_p, new_b_p, new_c_p, new_a_s, new_b_s, new_c_s = [], [], [], [], [], []
    h = None
    for l in range(depth):
        mod_s = mod_all[l, :n_dec]
        mod_p = mod_all[l, n_dec:n_dec + n_batch].reshape(n_batch, 1, N_MOD * d)
        ng = norm_gains[l]
        if l == 0:
            h = rowwise(x, mod_s, mod_p, norm=(ng[0], 1, 0))[0]
        if l % 2 == 0:
            e = l // 2
            proj = _mm(h, ab_w_in[e])
            y_s, g_s = _even_sample(proj, state_conv_a[e].transpose(1, 0, 2), state_pool_b[e].transpose(1, 0, 2),
                                    a_conv_w[e], b_pool_w[e], b_pool_scale[e], row_block=sample_block)
            ycat, sa, sb = _even_prompt(proj, y_s, a_conv_w[e], b_pool_w[e], b_pool_scale[e],
                                        n_batch=n_batch, seq=seq)
            y = _mm(ycat, ab_w_out[e])
            new_a_p.append(sa)
            new_b_p.append(sb)
            new_a_s.append(jnp.concatenate([state_conv_a[e][:, 1:], g_s[:, None]], axis=1))
            b_in_s = proj[n_prompt_rows:, 3 * (d // 2):]
            new_b_s.append(jnp.concatenate([state_pool_b[e][:, 1:], b_in_s[:, None]], axis=1))
        else:
            o = l // 2
            u = _mm(h, c_pw1_w[o], c_pw1_b[o])
            v_s, conv_s = _odd_sample(u, state_conv_c[o].transpose(1, 0, 2), c_dw_w[o], c_dw_b[o],
                                      row_block=sample_block)
            zcat, s_c = _odd_prompt(u, conv_s, c_dw_w[o], c_dw_b[o], c_ln_g[o], c_ln_b[o],
                                    n_batch=n_batch, seq=seq)
            y = _mm(zcat, c_pw2_w[o], c_pw2_b[o])
            new_c_p.append(s_c)
            new_c_s.append(jnp.concatenate([state_conv_c[o][:, 1:], v_s[:, None]], axis=1))
        x, h2, top_idx, gates = rowwise(x, mod_s, mod_p, resid=(y, ng[1], 2), norm=(ng[2], 4, 3),
                                        router=(moe_w_router[l], moe_b_router[l]), h_dtype=f32)
        y = _moe(h2, top_idx, gates, moe_w_gate_up, moe_b_gate_up, moe_w_down, moe_b_down, l)
        if l + 1 < depth:
            mod_s_n = mod_all[l + 1, :n_dec]
            mod_p_n = mod_all[l + 1, n_dec:n_dec + n_batch].reshape(n_batch, 1, N_MOD * d)
            x = rowwise(x, mod_s, mod_p, resid=(y, ng[3], 5))[0]
            h = rowwise(x, mod_s_n, mod_p_n, norm=(norm_gains[l + 1][0], 1, 0))[0]
        else:
            x = rowwise(x, mod_s, mod_p, resid=(y, ng[3], 5))[0]

    y_prompt = x[:n_prompt_rows].reshape(n_batch, seq, d)
    y_sample = x[n_prompt_rows:].reshape(n_dec, 1, d)
    return (y_prompt, y_sample, jnp.stack(new_a_p), jnp.stack(new_b_p), jnp.stack(new_c_p),
            jnp.stack(new_a_s), jnp.stack(new_b_s), jnp.stack(new_c_s))
```

```python
import functools

import jax
import jax.numpy as jnp
from jax import lax
from jax.experimental import pallas as pl
from jax.experimental.pallas import tpu as pltpu

f32 = jnp.float32
bf16 = jnp.bfloat16
i32 = jnp.int32

LANE = 128
V7X_VMEM_LIMIT_BYTES = 56 * 1024 * 1024

POOL_WINDOWS = (2, 4, 8, 16)
POOL_BUF = max(POOL_WINDOWS) - 1
CONV_A_K = 3
CONV_C_K = 31
TOP_K = 4
SWIGLU_LIMIT = 7.0
SWIGLU_ALPHA = 1.702
N_MOD = 6
RMS_EPS = 1e-6
LN_EPS = 1e-5
PAST_LEN = 16384

ROW_TILE = 128
MM_ROW_TILE = 640
MM_COL_TILE = 512
HALO_A = 16
HALO_C = 32
CONV_ROW_BLOCK = 64
LN_ROW_BLOCK = 16
MOE_CHUNK = 128
MOE_BIG = 4
MOE_SUPER = 9 * MOE_CHUNK
MOE_C1 = 512
MOE_C2 = 512
GATHER_ROWS = 256


def _params(sem, vmem=V7X_VMEM_LIMIT_BYTES):
    return pltpu.CompilerParams(dimension_semantics=sem, vmem_limit_bytes=vmem)


def _sigmoid(x):
    return 1.0 / (1.0 + jnp.exp(-x))


def _ada_body(c_ref, w_ref, b_ref, o_ref):
    c = c_ref[...]
    ca = (c * _sigmoid(c)).astype(bf16)
    o_ref[...] = jnp.dot(ca, w_ref[...].astype(bf16), preferred_element_type=f32) + b_ref[...]


def _ada(c_all, w_ada, b_ada):
    depth, d, n = w_ada.shape
    rows = c_all.shape[0]
    tn = MM_COL_TILE
    return pl.pallas_call(
        _ada_body,
        grid=(depth, n // tn),
        in_specs=[
            pl.BlockSpec((rows, d), lambda l, j: (0, 0)),
            pl.BlockSpec((None, d, tn), lambda l, j: (l, 0, j)),
            pl.BlockSpec((None, 1, tn), lambda l, j: (l, 0, j)),
        ],
        out_specs=pl.BlockSpec((None, rows, tn), lambda l, j: (l, 0, j)),
        out_shape=jax.ShapeDtypeStruct((depth, rows, n), f32),
        compiler_params=_params(("arbitrary", "arbitrary")),
        name="ada",
    )(c_all, w_ada, b_ada.reshape(depth, 1, n))


def _rms(x, g):
    return x * lax.rsqrt(jnp.mean(x * x, axis=-1, keepdims=True) + RMS_EPS) * g


def _rowwise_body(*refs, n_prompt_tiles, has_resid, has_norm, has_router):
    refs = list(refs)
    x_ref = refs.pop(0)
    is_sample = pl.program_id(0) == n_prompt_tiles

    def mod(ref_s, ref_p):
        return jnp.where(is_sample, ref_s[...], ref_p[0])

    x = x_ref[...]
    if has_resid:
        y_ref, gy_ref, gate_s, gate_p = refs[:4]
        refs = refs[4:]
        x = x + mod(gate_s, gate_p) * _rms(y_ref[...], gy_ref[...])
    if has_norm:
        gh_ref, scale_s, scale_p, shift_s, shift_p = refs[:5]
        refs = refs[5:]
        h = _rms(x, gh_ref[...]) * (1.0 + mod(scale_s, scale_p)) + mod(shift_s, shift_p)
    if has_router:
        wr_ref, br_ref = refs[:2]
        refs = refs[2:]
    outs = refs
    if has_resid:
        outs.pop(0)[...] = x
    if has_norm:
        h_ref = outs.pop(0)
        h_ref[...] = h.astype(h_ref.dtype)
    if has_router:
        idx_ref, gate_ref = outs
        logits = jnp.dot(h, wr_ref[...], preferred_element_type=f32, precision=lax.Precision.HIGHEST) + br_ref[...]
        n_exp = logits.shape[-1]
        lane = lax.broadcasted_iota(i32, logits.shape, 1)
        vals, idxs = [], []
        for _ in range(TOP_K):
            m = jnp.max(logits, axis=-1, keepdims=True)
            idx = jnp.min(jnp.where(logits == m, lane, n_exp), axis=-1, keepdims=True)
            vals.append(m)
            idxs.append(idx)
            logits = jnp.where(lane == idx, -jnp.inf, logits)
        es = [jnp.exp(v - vals[0]) for v in vals]
        denom = es[0] + es[1] + es[2] + es[3]
        out_lane = lax.broadcasted_iota(i32, idx_ref.shape, 1)
        idx_out = jnp.zeros(idx_ref.shape, i32)
        gate_out = jnp.zeros(gate_ref.shape, f32)
        for k in range(TOP_K):
            idx_out = jnp.where(out_lane == k, idxs[k], idx_out)
            gate_out = jnp.where(out_lane == k, es[k] / denom, gate_out)
        idx_ref[...] = idx_out
        gate_ref[...] = gate_out


def _rowwise(x, mod_s, mod_p, *, n_prompt_rows, rows_per_seq, resid=None, norm=None, router=None, h_dtype=bf16,
             norm_mods=None):
    r, d = x.shape
    tm = ROW_TILE
    n_prompt_tiles = n_prompt_rows // tm
    tiles_per_seq = rows_per_seq // tm
    n_batch = mod_p.shape[0]
    row_spec = pl.BlockSpec((tm, d), lambda i: (i, 0))
    vec_spec = pl.BlockSpec((1, d), lambda i: (0, 0))

    def mod_specs(col):
        return [
            pl.BlockSpec((mod_s.shape[0], d), lambda i: (0, col)),
            pl.BlockSpec((1, 1, d), lambda i: (jnp.minimum(i // tiles_per_seq, n_batch - 1), 0, col)),
        ]

    args, in_specs, out_shape, out_specs = [x], [row_spec], [], []
    if resid is not None:
        y, gain_y, gate_col = resid
        args += [y, gain_y.reshape(1, d), mod_s, mod_p]
        in_specs += [row_spec, vec_spec] + mod_specs(gate_col)
        out_shape.append(jax.ShapeDtypeStruct((r, d), f32))
        out_specs.append(row_spec)
    if norm is not None:
        gain_h, scale_col, shift_col = norm
        nm_s, nm_p = (mod_s, mod_p) if norm_mods is None else norm_mods
        args += [gain_h.reshape(1, d), nm_s, nm_p, nm_s, nm_p]
        in_specs += [vec_spec] + mod_specs(scale_col) + mod_specs(shift_col)
        out_shape.append(jax.ShapeDtypeStruct((r, d), h_dtype))
        out_specs.append(row_spec)
    if router is not None:
        w_router, b_router = router
        n_exp = w_router.shape[1]
        args += [w_router, b_router.reshape(1, n_exp)]
        in_specs += [pl.BlockSpec((d, n_exp), lambda i: (0, 0)), pl.BlockSpec((1, n_exp), lambda i: (0, 0))]
        out_shape += [jax.ShapeDtypeStruct((r, LANE), i32), jax.ShapeDtypeStruct((r, LANE), f32)]
        out_specs += [pl.BlockSpec((tm, LANE), lambda i: (i, 0))] * 2
    body = functools.partial(_rowwise_body, n_prompt_tiles=n_prompt_tiles, has_resid=resid is not None,
                             has_norm=norm is not None, has_router=router is not None)
    return pl.pallas_call(
        body, grid=(r // tm,), in_specs=in_specs, out_specs=out_specs, out_shape=out_shape,
        compiler_params=_params(("arbitrary",)), name="rowwise",
    )(*args)


def _mm_body(*refs, has_bias):
    if has_bias:
        x_ref, w_ref, b_ref, o_ref, wb_ref = refs
    else:
        x_ref, w_ref, o_ref, wb_ref = refs

    @pl.when(pl.program_id(1) == 0)
    def _():
        wb_ref[...] = w_ref[...].astype(bf16)

    acc = jnp.dot(x_ref[...], wb_ref[...], preferred_element_type=f32)
    if has_bias:
        acc = acc + b_ref[...]
    o_ref[...] = acc


def _mm(x, w, bias=None):
    m, k = x.shape
    n = w.shape[1]
    tm, tn = MM_ROW_TILE, MM_COL_TILE
    args = [x, w]
    in_specs = [pl.BlockSpec((tm, k), lambda j, i: (i, 0)), pl.BlockSpec((k, tn), lambda j, i: (0, j))]
    if bias is not None:
        args.append(bias.reshape(1, n))
        in_specs.append(pl.BlockSpec((1, tn), lambda j, i: (0, j)))
    return pl.pallas_call(
        functools.partial(_mm_body, has_bias=bias is not None),
        grid=(n // tn, m // tm),
        in_specs=in_specs,
        out_specs=pl.BlockSpec((tm, tn), lambda j, i: (i, j)),
        out_shape=jax.ShapeDtypeStruct((m, n), f32),
        scratch_shapes=[pltpu.VMEM((k, tn), bf16)],
        compiler_params=_params(("arbitrary", "arbitrary")),
        name="mm",
    )(*args)


def _pool_mix(z, pw_ref, ps_ref, gi, c0, cw):
    yb = jnp.dot(z.astype(bf16), pw_ref[gi].astype(bf16), preferred_element_type=f32)
    return yb * ps_ref[:, c0:c0 + cw]


def _even_prompt_body(p_ref, ph_ref, ysm_ref, cw_ref, pw_ref, ps_ref, y_ref, sa_ref, sb_ref, eg_ref, eb_ref,
                      *, tl, da, tiles, n_tiles):
    step = pl.program_id(0)

    @pl.when(step == n_tiles)
    def _sample_rows():
        y_ref[...] = ysm_ref[...]

    @pl.when(step < n_tiles)
    def _prompt_rows():
        _even_prompt_tile(p_ref, ph_ref, cw_ref, pw_ref, ps_ref, y_ref, sa_ref, sb_ref, eg_ref, eb_ref,
                          tl=tl, da=da, tile_in_seq=step % tiles)


def _even_prompt_tile(p_ref, ph_ref, cw_ref, pw_ref, ps_ref, y_ref, sa_ref, sb_ref, eg_ref, eb_ref,
                      *, tl, da, tile_in_seq):
    first = tile_in_seq == 0
    a_pre, a_post = p_ref[:, 0:da], p_ref[:, da:2 * da]
    a_val, b_in = p_ref[:, 2 * da:3 * da], p_ref[:, 3 * da:4 * da]
    g = a_pre * a_val
    eg_ref[0:HALO_A, :] = jnp.where(first, 0.0, ph_ref[:, 0:da] * ph_ref[:, 2 * da:3 * da])
    eg_ref[HALO_A:, :] = g
    eb_ref[0:HALO_A, :] = jnp.where(first, 0.0, ph_ref[:, 3 * da:4 * da])
    eb_ref[HALO_A:, :] = b_in
    conv = (cw_ref[0:1, :] * eg_ref[HALO_A - 2:HALO_A - 2 + tl, :]
            + cw_ref[1:2, :] * eg_ref[HALO_A - 1:HALO_A - 1 + tl, :]
            + cw_ref[2:3, :] * g)
    y_ref[:, 0:da] = (a_post * conv).astype(y_ref.dtype)
    sa_ref[...] = eg_ref[HALO_A + tl - (CONV_A_K - 1):HALO_A + tl, :]
    sb_ref[...] = eb_ref[HALO_A + tl - POOL_BUF:HALO_A + tl, :]
    pos = tile_in_seq * tl + lax.broadcasted_iota(i32, (tl, 1), 0)
    cw = da // len(POOL_WINDOWS)
    for gi, w in enumerate(POOL_WINDOWS):
        c0 = gi * cw
        cur = b_in[:, c0:c0 + cw]
        s = cur
        for back in range(1, w):
            s = s + eb_ref[HALO_A - back:HALO_A - back + tl, c0:c0 + cw]
        count = jnp.minimum(pos + 1, w).astype(f32)
        z = s / count - cur
        y_ref[:, da + c0:da + c0 + cw] = _pool_mix(z, pw_ref, ps_ref, gi, c0, cw).astype(y_ref.dtype)


def _even_prompt(proj, y_sample, conv_w, pool_w, pool_scale, *, n_batch, seq):
    d_in = proj.shape[1]
    da = d_in // 4
    tl = ROW_TILE
    tiles = seq // tl
    n_tiles = n_batch * tiles
    halo_per_tile = tl // HALO_A
    ng, cg = pool_w.shape[0], pool_w.shape[1]
    assert y_sample.shape == (tl, 2 * da)

    def tile(s):
        return jnp.minimum(s, n_tiles - 1)

    def batch(s):
        return tile(s) // tiles

    return pl.pallas_call(
        functools.partial(_even_prompt_body, tl=tl, da=da, tiles=tiles, n_tiles=n_tiles),
        grid=(n_tiles + 1,),
        in_specs=[
            pl.BlockSpec((tl, d_in), lambda s: (tile(s), 0)),
            pl.BlockSpec((HALO_A, d_in), lambda s: (jnp.maximum(tile(s) * halo_per_tile - 1, 0), 0)),
            pl.BlockSpec((tl, 2 * da), lambda s: (0, 0)),
            pl.BlockSpec((CONV_A_K, da), lambda s: (0, 0)),
            pl.BlockSpec((ng, cg, cg), lambda s: (0, 0, 0)),
            pl.BlockSpec((1, da), lambda s: (0, 0)),
        ],
        out_specs=[
            pl.BlockSpec((tl, 2 * da), lambda s: (s, 0)),
            pl.BlockSpec((None, CONV_A_K - 1, da), lambda s: (batch(s), 0, 0)),
            pl.BlockSpec((None, POOL_BUF, da), lambda s: (batch(s), 0, 0)),
        ],
        out_shape=[
            jax.ShapeDtypeStruct(((n_tiles + 1) * tl, 2 * da), bf16),
            jax.ShapeDtypeStruct((n_batch, CONV_A_K - 1, da), f32),
            jax.ShapeDtypeStruct((n_batch, POOL_BUF, da), f32),
        ],
        scratch_shapes=[pltpu.VMEM((HALO_A + tl, da), f32), pltpu.VMEM((HALO_A + tl, da), f32)],
        compiler_params=_params(("arbitrary",)),
        name="even_prompt",
    )(proj, proj, y_sample, conv_w, pool_w, pool_scale.reshape(1, da))


def _even_sample_body(apre_ref, apost_ref, aval_ref, bin_ref, sa_ref, sb_ref, cw_ref, pw_ref, ps_ref,
                      y_ref, g_ref, *, ng):
    s_id = pl.program_id(0)

    @pl.when(s_id < ng)
    def _conv():
        g = apre_ref[...] * aval_ref[...]
        g_ref[...] = g
        conv = cw_ref[0:1, :] * sa_ref[0] + cw_ref[1:2, :] * sa_ref[1] + cw_ref[2:3, :] * g
        y_ref[...] = (apost_ref[...] * conv).astype(y_ref.dtype)

    @pl.when(s_id >= ng)
    def _pool():
        cur = bin_ref[...]
        s = cur
        z = jnp.zeros_like(cur)
        back = 1
        for gi, w in enumerate(POOL_WINDOWS):
            while back < w:
                s = s + sb_ref[POOL_BUF - back]
                back += 1
            count = float(min(PAST_LEN + 1, w))
            z = jnp.where(s_id - ng == gi, s / count - cur, z)
        yb = jnp.dot(z.astype(bf16), pw_ref[...].astype(bf16), preferred_element_type=f32) * ps_ref[...]
        y_ref[...] = yb.astype(y_ref.dtype)


def _even_sample(proj, state_a_t, state_b_t, conv_w, pool_w, pool_scale, *, row_block):
    da = proj.shape[1] // 4
    ng, cg = pool_w.shape[0], pool_w.shape[1]
    nb = state_a_t.shape[1]
    assert nb == ROW_TILE and cg * ng == da and ng == len(POOL_WINDOWS)

    def conv_chunk(s):
        return jnp.minimum(s, ng - 1)

    def pool_group(s):
        return jnp.maximum(s - ng, 0)

    def col(k):
        return pl.BlockSpec((nb, cg), lambda s: (row_block, k * ng + conv_chunk(s)))

    return pl.pallas_call(
        functools.partial(_even_sample_body, ng=ng),
        grid=(2 * ng,),
        in_specs=[
            col(0), col(1), col(2),
            pl.BlockSpec((nb, cg), lambda s: (row_block, 3 * ng + pool_group(s))),
            pl.BlockSpec((CONV_A_K - 1, nb, cg), lambda s: (0, 0, conv_chunk(s))),
            pl.BlockSpec((POOL_BUF, nb, cg), lambda s: (0, 0, pool_group(s))),
            pl.BlockSpec((CONV_A_K, cg), lambda s: (0, conv_chunk(s))),
            pl.BlockSpec((None, cg, cg), lambda s: (pool_group(s), 0, 0)),
            pl.BlockSpec((1, cg), lambda s: (0, pool_group(s))),
        ],
        out_specs=[
            pl.BlockSpec((nb, cg), lambda s: (0, s)),
            pl.BlockSpec((nb, cg), lambda s: (0, conv_chunk(s))),
        ],
        out_shape=[
            jax.ShapeDtypeStruct((nb, 2 * da), bf16),
            jax.ShapeDtypeStruct((nb, da), f32),
        ],
        compiler_params=_params(("arbitrary",)),
        name="even_sample",
    )(proj, proj, proj, proj, state_a_t, state_b_t, conv_w, pool_w, pool_scale.reshape(1, da))


def _ln_swish(src_ref, dst_ref, g_ref, b_ref):
    rows = LN_ROW_BLOCK

    def block(i, carry):
        r = pl.ds(pl.multiple_of(i * rows, rows), rows)
        mu = jnp.mean(src_ref[r, :], axis=-1, keepdims=True)
        zc = src_ref[r, :] - mu
        var = jnp.mean(zc * zc, axis=-1, keepdims=True)
        y = (src_ref[r, :] - mu) * lax.rsqrt(var + LN_EPS) * g_ref[...] + b_ref[...]
        dst_ref[r, :] = (y * _sigmoid(y)).astype(dst_ref.dtype)
        return carry

    lax.fori_loop(0, src_ref.shape[0] // rows, block, 0)


def _odd_prompt_body(u_ref, uh_ref, zsm_ref, dw_ref, db_ref, lg_ref, lb_ref, z_ref, sc_ref, ext_ref, acc_ref, sh_ref,
                     *, tl, dc, tiles, n_tiles):
    step = pl.program_id(0)

    @pl.when(step == n_tiles)
    def _sample_rows():
        _ln_swish(zsm_ref, z_ref, lg_ref, lb_ref)

    @pl.when(step < n_tiles)
    def _prompt_rows():
        _odd_prompt_tile(u_ref, uh_ref, dw_ref, db_ref, lg_ref, lb_ref, z_ref, sc_ref, ext_ref, acc_ref, sh_ref,
                         tl=tl, dc=dc, first=step % tiles == 0)


def _odd_prompt_tile(u_ref, uh_ref, dw_ref, db_ref, lg_ref, lb_ref, z_ref, sc_ref, ext_ref, acc_ref, sh_ref,
                     *, tl, dc, first):
    v = u_ref[:, 0:dc] * _sigmoid(u_ref[:, dc:2 * dc])
    vh = uh_ref[:, 0:dc] * _sigmoid(uh_ref[:, dc:2 * dc])
    ext_ref[0:HALO_C, :] = jnp.where(first, 0.0, vh)
    ext_ref[HALO_C:, :] = v
    off = HALO_C - (CONV_C_K - 1)
    rb, a_max = CONV_ROW_BLOCK, (off + CONV_C_K - 1) // 8
    sh_rows = sh_ref.shape[1]
    for b in range(1, 8):
        sh_ref[b - 1] = ext_ref[b:b + sh_rows, :]
    for r0 in range(0, tl, rb):
        for c0 in range(0, dc, LANE):
            lanes = slice(c0, c0 + LANE)
            acc = jnp.broadcast_to(db_ref[:, lanes], (rb, LANE))
            for b in range(8):
                for a in range(a_max + 1):
                    k = 8 * a + b - off
                    if 0 <= k < CONV_C_K:
                        rows = slice(r0 + 8 * a, r0 + 8 * a + rb)
                        win = ext_ref[rows, lanes] if b == 0 else sh_ref[b - 1, rows, lanes]
                        acc = acc + dw_ref[k:k + 1, lanes] * win
            acc_ref[r0:r0 + rb, lanes] = acc
    _ln_swish(acc_ref, z_ref, lg_ref, lb_ref)
    sc_ref[...] = ext_ref[HALO_C + tl - (CONV_C_K - 1):HALO_C + tl, :]


def _odd_prompt(u, conv_sample, dw_w, dw_b, ln_g, ln_b, *, n_batch, seq):
    dc = u.shape[1] // 2
    tl = ROW_TILE
    tiles = seq // tl
    n_tiles = n_batch * tiles
    halo_per_tile = tl // HALO_C
    assert conv_sample.shape == (tl, dc)
    vec = pl.BlockSpec((1, dc), lambda s: (0, 0))

    def tile(s):
        return jnp.minimum(s, n_tiles - 1)

    return pl.pallas_call(
        functools.partial(_odd_prompt_body, tl=tl, dc=dc, tiles=tiles, n_tiles=n_tiles),
        grid=(n_tiles + 1,),
        in_specs=[
            pl.BlockSpec((tl, 2 * dc), lambda s: (tile(s), 0)),
            pl.BlockSpec((HALO_C, 2 * dc), lambda s: (jnp.maximum(tile(s) * halo_per_tile - 1, 0), 0)),
            pl.BlockSpec((tl, dc), lambda s: (0, 0)),
            pl.BlockSpec((CONV_C_K, dc), lambda s: (0, 0)),
            vec, vec, vec,
        ],
        out_specs=[
            pl.BlockSpec((tl, dc), lambda s: (s, 0)),
            pl.BlockSpec((None, CONV_C_K - 1, dc), lambda s: (tile(s) // tiles, 0, 0)),
        ],
        out_shape=[
            jax.ShapeDtypeStruct(((n_tiles + 1) * tl, dc), bf16),
            jax.ShapeDtypeStruct((n_batch, CONV_C_K - 1, dc), f32),
        ],
        scratch_shapes=[pltpu.VMEM((HALO_C + tl, dc), f32), pltpu.VMEM((tl, dc), f32),
                        pltpu.VMEM((7, tl + HALO_C - 8, dc), f32)],
        compiler_params=_params(("arbitrary",)),
        name="odd_prompt",
    )(u, u, conv_sample, dw_w, dw_b.reshape(1, dc), ln_g.reshape(1, dc), ln_b.reshape(1, dc))


def _odd_sample_conv_body(val_ref, gate_ref, sc_ref, dw_ref, db_ref, v_ref, a_ref):
    v = val_ref[...] * _sigmoid(gate_ref[...])
    v_ref[...] = v
    acc = dw_ref[CONV_C_K - 1:CONV_C_K, :] * v + db_ref[...]
    for k in range(CONV_C_K - 1):
        acc = acc + dw_ref[k:k + 1, :] * sc_ref[k]
    a_ref[...] = acc


def _odd_sample(u, state_c_t, dw_w, dw_b, *, row_block):
    dc = u.shape[1] // 2
    nb = state_c_t.shape[1]
    cc = MM_COL_TILE
    nchunk = dc // cc
    return pl.pallas_call(
        _odd_sample_conv_body,
        grid=(nchunk,),
        in_specs=[
            pl.BlockSpec((nb, cc), lambda j: (row_block, j)),
            pl.BlockSpec((nb, cc), lambda j: (row_block, nchunk + j)),
            pl.BlockSpec((CONV_C_K - 1, nb, cc), lambda j: (0, 0, j)),
            pl.BlockSpec((CONV_C_K, cc), lambda j: (0, j)),
            pl.BlockSpec((1, cc), lambda j: (0, j)),
        ],
        out_specs=[pl.BlockSpec((nb, cc), lambda j: (0, j))] * 2,
        out_shape=[jax.ShapeDtypeStruct((nb, dc), f32)] * 2,
        compiler_params=_params(("arbitrary",)),
        name="odd_sample_conv",
    )(u, u, state_c_t, dw_w, dw_b.reshape(1, dc))


def _gather_body(src_ref, h_hbm, o_ref, buf_ref, sem):
    i, n = pl.program_id(0), pl.num_programs(0)

    def row_copy(blk, r):
        slot = lax.rem(blk, 2)
        tok = src_ref[blk * GATHER_ROWS + r]
        return pltpu.make_async_copy(h_hbm.at[pl.ds(tok, 1), :], buf_ref.at[slot, pl.ds(r, 1), :], sem.at[slot])

    def start_block(blk):
        def body(r, carry):
            row_copy(blk, r).start()
            return carry
        lax.fori_loop(0, GATHER_ROWS, body, 0, unroll=8)

    @pl.when(i == 0)
    def _():
        start_block(0)

    @pl.when(i + 1 < n)
    def _():
        start_block(i + 1)

    def wait(r, carry):
        row_copy(i, r).wait()
        return carry

    lax.fori_loop(0, GATHER_ROWS, wait, 0, unroll=8)
    o_ref[...] = buf_ref[lax.rem(i, 2)].astype(o_ref.dtype)


def _moe_gather(h, src_tok):
    p_total = src_tok.shape[0]
    d = h.shape[1]
    return pl.pallas_call(
        _gather_body,
        grid_spec=pltpu.PrefetchScalarGridSpec(
            num_scalar_prefetch=1,
            grid=(p_total // GATHER_ROWS,),
            in_specs=[pl.BlockSpec(memory_space=pl.ANY)],
            out_specs=pl.BlockSpec((GATHER_ROWS, d), lambda i, src: (i, 0)),
            scratch_shapes=[pltpu.VMEM((2, GATHER_ROWS, d), f32), pltpu.SemaphoreType.DMA((2,))],
        ),
        out_shape=jax.ShapeDtypeStruct((p_total, d), bf16),
        compiler_params=_params(("arbitrary",)),
        name="moe_gather",
    )(src_tok, h)


def _pair_act(h):
    n = h.shape[1]
    glu = jnp.minimum(h, SWIGLU_LIMIT)
    lin = jnp.clip(h, -SWIGLU_LIMIT, SWIGLU_LIMIT) + 1.0
    lin_next = pltpu.roll(lin, n - 1, axis=1)
    return glu * _sigmoid(SWIGLU_ALPHA * glu) * lin_next


def _expert_body(se_ref, sblk_ref, snc_ref, tail_ref, x_ref, w1_ref, b1_ref, wd_ref, bd_ref, ys_hbm,
                 w1b_ref, wdb_ref, act_ref, perm_ref, obuf_ref, osem, *, j1, k2):
    del se_ref
    i, j = pl.program_id(0), pl.program_id(1)
    n_chunks = snc_ref[i]
    half = MOE_C1 // 2
    ch = MOE_CHUNK
    row0 = sblk_ref[i] * ch

    n_big = n_chunks // MOE_BIG
    big = MOE_BIG * ch

    def for_row_blocks(fn):
        def big_body(b, carry):
            fn(pl.multiple_of(b * big, big), big)
            return carry

        def small_body(c, carry):
            fn(pl.multiple_of(c * ch, ch), ch)
            return carry

        lax.fori_loop(0, n_big, big_body, 0)
        lax.fori_loop(n_big * MOE_BIG, n_chunks, small_body, 0)

    def out_copy(src_slot, src_row, rows, dst_row, col_tile):
        return pltpu.make_async_copy(
            obuf_ref.at[src_slot, pl.ds(src_row, rows), :],
            ys_hbm.at[pl.ds(pl.multiple_of(dst_row, ch), rows), pl.ds(pl.multiple_of(col_tile * MOE_C2, MOE_C2), MOE_C2)],
            osem.at[src_slot])

    def wait_tile(slot, col_tile):
        for_row_blocks(lambda r0, rows: out_copy(slot, r0, rows, row0 + r0, col_tile).wait())

    @pl.when(j < j1)
    def _gate_up():
        w1b_ref[...] = w1_ref[...].astype(bf16)

        def rows_fn(r0, rows):
            even = (lax.broadcasted_iota(i32, (rows, half), 1) & 1) == 0
            hh = jnp.dot(x_ref[pl.ds(r0, rows), :], w1b_ref[...], preferred_element_type=f32) + b1_ref[...]
            pa = _pair_act(hh[:, :half])
            pb = _pair_act(hh[:, half:])
            act_ref[j, pl.ds(r0, rows), :] = jnp.where(even, pa, pltpu.roll(pb, 1, axis=1)).astype(bf16)

        for_row_blocks(rows_fn)

    @pl.when(j >= j1)
    def _down():
        k = j - j1
        slot = lax.rem(k, 2)

        @pl.when(k >= 2)
        def _():
            wait_tile(slot, k - 2)

        for jj in range(j1):
            for c in range(MOE_C2 // LANE):
                rows = slice(jj * half, (jj + 1) * half)
                cols = slice(c * LANE, (c + 1) * LANE)
                perm_ref[c, pl.ds(0, half // 2, stride=2), :] = wd_ref[jj * half:jj * half + half // 2, cols]
                perm_ref[c, pl.ds(1, half // 2, stride=2), :] = wd_ref[jj * half + half // 2:(jj + 1) * half, cols]
                wdb_ref[rows, cols] = perm_ref[c].astype(bf16)

        def rows_fn(r0, rows):
            a = jnp.concatenate([act_ref[jj, pl.ds(r0, rows), :] for jj in range(j1)], axis=1)
            obuf_ref[slot, pl.ds(r0, rows), :] = jnp.dot(a, wdb_ref[...], preferred_element_type=f32) + bd_ref[...]
            out_copy(slot, r0, rows, row0 + r0, k).start()

        for_row_blocks(rows_fn)

        @pl.when(k == k2 - 1)
        def _drain():
            wait_tile(1 - slot, k - 1)
            wait_tile(slot, k)

            @pl.when(i == pl.num_programs(0) - 1)
            def _tail():
                obuf_ref[0, 0:ch, :] = jnp.zeros((ch, MOE_C2), f32)
                tail_row0, n_tail = tail_ref[0] * ch, tail_ref[1]

                def start(t, carry):
                    for kk in range(k2):
                        out_copy(0, 0, ch, tail_row0 + t * ch, kk).start()
                    return carry

                def wait(t, carry):
                    for kk in range(k2):
                        out_copy(0, 0, ch, tail_row0 + t * ch, kk).wait()
                    return carry

                lax.fori_loop(0, n_tail, start, 0)
                lax.fori_loop(0, n_tail, wait, 0)


def _moe_experts(xs, w_gate_up, b_gate_up, w_down, b_down, layer, ys_rows, n_super, sup_e, sup_blk, sup_nc, tail):
    d = xs.shape[1]
    de = w_down.shape[2]
    j1 = (2 * de) // MOE_C1
    k2 = d // MOE_C2
    assert k2 >= 2
    s = MOE_SUPER
    el = pl.Element

    def w1_idx(i, j, se, sb, nc, tl):
        return (layer, se[i], 0, jnp.minimum(j, j1 - 1))

    def wd_idx(i, j, se, sb, nc, tl):
        return (layer, se[i], 0, jnp.maximum(j - j1, 0))

    return pl.pallas_call(
        functools.partial(_expert_body, j1=j1, k2=k2),
        grid_spec=pltpu.PrefetchScalarGridSpec(
            num_scalar_prefetch=4,
            grid=(n_super, j1 + k2),
            in_specs=[
                pl.BlockSpec((el(s), el(d)), lambda i, j, se, sb, nc, tl: (sb[i] * MOE_CHUNK, 0),
                             pipeline_mode=pl.Buffered(1)),
                pl.BlockSpec((None, None, d, MOE_C1), w1_idx),
                pl.BlockSpec((None, None, 1, MOE_C1), w1_idx),
                pl.BlockSpec((None, None, de, MOE_C2), wd_idx),
                pl.BlockSpec((None, None, 1, MOE_C2), wd_idx),
            ],
            out_specs=pl.BlockSpec(memory_space=pl.ANY),
            scratch_shapes=[
                pltpu.VMEM((d, MOE_C1), bf16),
                pltpu.VMEM((de, MOE_C2), bf16),
                pltpu.VMEM((j1, s, MOE_C1 // 2), bf16),
                pltpu.VMEM((MOE_C2 // LANE, MOE_C1 // 2, LANE), f32),
                pltpu.VMEM((2, s, MOE_C2), f32),
                pltpu.SemaphoreType.DMA((2,)),
            ],
        ),
        out_shape=jax.ShapeDtypeStruct((ys_rows, d), f32),
        compiler_params=_params(("arbitrary", "arbitrary")),
        name="moe_experts",
    )(sup_e, sup_blk, sup_nc, tail, xs, w_gate_up, b_gate_up.reshape(*b_gate_up.shape[:2], 1, -1),
      w_down, b_down.reshape(*b_down.shape[:2], 1, -1))


def _combine_body(dest_ref, ys_hbm, gate_ref, o_ref, buf_ref, sem):
    tm = o_ref.shape[0]
    i, n = pl.program_id(0), pl.num_programs(0)

    def row_copy(tile, r, k):
        slot = lax.rem(tile, 2)
        p = dest_ref[(tile * tm + r) * TOP_K + k]
        return pltpu.make_async_copy(ys_hbm.at[pl.ds(p, 1), :], buf_ref.at[slot, k, pl.ds(r, 1), :], sem.at[slot])

    def start_tile(tile):
        def body(r, carry):
            for k in range(TOP_K):
                row_copy(tile, r, k).start()
            return carry
        lax.fori_loop(0, tm, body, 0, unroll=4)

    @pl.when(i == 0)
    def _():
        start_tile(0)

    @pl.when(i + 1 < n)
    def _():
        start_tile(i + 1)

    def wait(r, carry):
        for k in range(TOP_K):
            row_copy(i, r, k).wait()
        return carry

    lax.fori_loop(0, tm, wait, 0, unroll=4)
    slot = lax.rem(i, 2)
    g = gate_ref[...]
    acc = buf_ref[slot, 0] * g[:, 0:1]
    for k in range(1, TOP_K):
        acc = acc + buf_ref[slot, k] * g[:, k:k + 1]
    o_ref[...] = acc


def _moe_combine(ys, dest, gates, n_rows):
    d = ys.shape[1]
    tm = ROW_TILE
    return pl.pallas_call(
        _combine_body,
        grid_spec=pltpu.PrefetchScalarGridSpec(
            num_scalar_prefetch=1,
            grid=(n_rows // tm,),
            in_specs=[pl.BlockSpec(memory_space=pl.ANY), pl.BlockSpec((tm, LANE), lambda i, dest: (i, 0))],
            out_specs=pl.BlockSpec((tm, d), lambda i, dest: (i, 0)),
            scratch_shapes=[pltpu.VMEM((2, TOP_K, tm, d), f32), pltpu.SemaphoreType.DMA((2,))],
        ),
        out_shape=jax.ShapeDtypeStruct((n_rows, d), f32),
        compiler_params=_params(("arbitrary",)),
        name="moe_combine",
    )(dest, ys, gates)


def _moe_plan(top_idx, n_exp):
    t = top_idx.shape[0]
    n_assign = t * TOP_K
    flat_e = top_idx.reshape(-1)
    order = jnp.argsort(flat_e).astype(i32)
    rank = jnp.argsort(order).astype(i32)
    counts = jnp.sum((flat_e[:, None] == jnp.arange(n_exp, dtype=i32)[None, :]).astype(i32), axis=0)
    padded = (counts + MOE_CHUNK - 1) // MOE_CHUNK * MOE_CHUNK
    start = jnp.cumsum(counts) - counts
    pstart = jnp.cumsum(padded) - padded
    dest = (pstart[flat_e] + rank - start[flat_e]).astype(i32)
    max_rows = -(-(n_assign + n_exp * (MOE_CHUNK - 1)) // MOE_CHUNK) * MOE_CHUNK
    p_total = -(-(max_rows + MOE_SUPER) // GATHER_ROWS) * GATHER_ROWS
    p = jnp.arange(p_total, dtype=i32)
    e_of_p = jnp.minimum(jnp.sum((p[:, None] >= (pstart + padded)[None, :]).astype(i32), axis=1), n_exp - 1)
    in_group = p - pstart[e_of_p]
    src_assign = order[jnp.clip(start[e_of_p] + in_group, 0, n_assign - 1)]
    src_tok = jnp.where(in_group < counts[e_of_p], src_assign // TOP_K, 0).astype(i32)
    n_sup_e = (padded + MOE_SUPER - 1) // MOE_SUPER
    cum = jnp.cumsum(n_sup_e)
    max_super = n_exp + max_rows // MOE_SUPER
    sid = jnp.arange(max_super, dtype=i32)
    sup_e = jnp.minimum(jnp.searchsorted(cum, sid, side="right"), n_exp - 1).astype(i32)
    within = sid - (cum[sup_e] - n_sup_e[sup_e])
    row0 = pstart[sup_e] + within * MOE_SUPER
    rows = jnp.clip(padded[sup_e] - within * MOE_SUPER, 0, MOE_SUPER)
    rows = jnp.where(sid < cum[-1], rows, 0)
    used = jnp.sum(padded)
    tail = jnp.stack([used // MOE_CHUNK, (max_rows - used) // MOE_CHUNK]).astype(i32)
    return dict(dest=dest, src_tok=src_tok, n_super=cum[-1].astype(i32), sup_e=sup_e, ys_rows=max_rows,
                sup_blk=(row0 // MOE_CHUNK).astype(i32), sup_nc=(rows // MOE_CHUNK).astype(i32), tail=tail)


def _moe(h, top_idx, gates, w_gate_up, b_gate_up, w_down, b_down, layer):
    plan = _moe_plan(top_idx[:, :TOP_K], w_down.shape[1])
    xs = _moe_gather(h, plan["src_tok"])
    ys = _moe_experts(xs, w_gate_up, b_gate_up, w_down, b_down, layer, plan["ys_rows"],
                      plan["n_super"], plan["sup_e"], plan["sup_blk"], plan["sup_nc"], plan["tail"])
    return _moe_combine(ys, plan["dest"], gates, h.shape[0])


def kernel(x_prompt, x_sample, c_prompt, c_sample, state_conv_a, state_pool_b, state_conv_c, w_ada, b_ada, norm_gains, ab_w_in, a_conv_w, b_pool_w, b_pool_scale, ab_w_out, c_pw1_w, c_pw1_b, c_dw_w, c_dw_b, c_ln_g, c_ln_b, c_pw2_w, c_pw2_b, moe_w_router, moe_b_router, moe_w_gate_up, moe_b_gate_up, moe_w_down, moe_b_down):
    n_batch, seq, d = x_prompt.shape
    n_dec, dec_seq, _ = x_sample.shape
    depth = w_ada.shape[0]
    assert dec_seq == 1 and n_dec == ROW_TILE and seq % ROW_TILE == 0
    n_prompt_rows = n_batch * seq
    r = n_prompt_rows + n_dec
    assert r % MM_ROW_TILE == 0
    sample_block = n_prompt_rows // ROW_TILE

    x = jnp.concatenate([x_prompt.reshape(n_prompt_rows, d), x_sample.reshape(n_dec, d)], axis=0)
    pad = (-(n_dec + n_batch)) % 8
    c_all = jnp.concatenate([c_sample, c_prompt, jnp.zeros((pad, d), f32)], axis=0)
    mod_all = _ada(c_all, w_ada, b_ada)
    rowwise = functools.partial(_rowwise, n_prompt_rows=n_prompt_rows, rows_per_seq=seq)

    new_a_p, new_b_p, new_c_p, new_a_s, new_b_s, new_c_s = [], [], [], [], [], []
    h = None
    for l in range(depth):
        mod_s = mod_all[l, :n_dec]
        mod_p = mod_all[l, n_dec:n_dec + n_batch].reshape(n_batch, 1, N_MOD * d)
        ng = norm_gains[l]
        if l == 0:
            h = rowwise(x, mod_s, mod_p, norm=(ng[0], 1, 0))[0]
        if l % 2 == 0:
            e = l // 2
            proj = _mm(h, ab_w_in[e])
            y_s, g_s = _even_sample(proj, state_conv_a[e].transpose(1, 0, 2), state_pool_b[e].transpose(1, 0, 2),
                                    a_conv_w[e], b_pool_w[e], b_pool_scale[e], row_block=sample_block)
            ycat, sa, sb = _even_prompt(proj, y_s, a_conv_w[e], b_pool_w[e], b_pool_scale[e],
                                        n_batch=n_batch, seq=seq)
            y = _mm(ycat, ab_w_out[e])
            new_a_p.append(sa)
            new_b_p.append(sb)
            new_a_s.append(jnp.concatenate([state_conv_a[e][:, 1:], g_s[:, None]], axis=1))
            b_in_s = proj[n_prompt_rows:, 3 * (d // 2):]
            new_b_s.append(jnp.concatenate([state_pool_b[e][:, 1:], b_in_s[:, None]], axis=1))
        else:
            o = l // 2
            u = _mm(h, c_pw1_w[o], c_pw1_b[o])
            v_s, conv_s = _odd_sample(u, state_conv_c[o].transpose(1, 0, 2), c_dw_w[o], c_dw_b[o],
                                      row_block=sample_block)
            zcat, s_c = _odd_prompt(u, conv_s, c_dw_w[o], c_dw_b[o], c_ln_g[o], c_ln_b[o],
                                    n_batch=n_batch, seq=seq)
            y = _mm(zcat, c_pw2_w[o], c_pw2_b[o])
            new_c_p.append(s_c)
            new_c_s.append(jnp.concatenate([state_conv_c[o][:, 1:], v_s[:, None]], axis=1))
        x, h2, top_idx, gates = rowwise(x, mod_s, mod_p, resid=(y, ng[1], 2), norm=(ng[2], 4, 3),
                                        router=(moe_w_router[l], moe_b_router[l]), h_dtype=f32)
        y = _moe(h2, top_idx, gates, moe_w_gate_up, moe_b_gate_up, moe_w_down, moe_b_down, l)
        if l + 1 < depth:
            mod_s_n = mod_all[l + 1, :n_dec]
            mod_p_n = mod_all[l + 1, n_dec:n_dec + n_batch].reshape(n_batch, 1, N_MOD * d)
            x, h = rowwise(x, mod_s, mod_p, resid=(y, ng[3], 5), norm=(norm_gains[l + 1][0], 1, 0),
                           norm_mods=(mod_s_n, mod_p_n))
        else:
            x = rowwise(x, mod_s, mod_p, resid=(y, ng[3], 5))[0]

    y_prompt = x[:n_prompt_rows].reshape(n_batch, seq, d)
    y_sample = x[n_prompt_rows:].reshape(n_dec, 1, d)
    return (y_prompt, y_sample, jnp.stack(new_a_p), jnp.stack(new_b_p), jnp.stack(new_c_p),
            jnp.stack(new_a_s), jnp.stack(new_b_s), jnp.stack(new_c_s))
```

```python
import functools

import jax
import jax.numpy as jnp
from jax import lax
from jax.experimental import pallas as pl
from jax.experimental.pallas import tpu as pltpu

f32 = jnp.float32
bf16 = jnp.bfloat16
i32 = jnp.int32

LANE = 128
V7X_VMEM_LIMIT_BYTES = 56 * 1024 * 1024

POOL_WINDOWS = (2, 4, 8, 16)
POOL_BUF = max(POOL_WINDOWS) - 1
CONV_A_K = 3
CONV_C_K = 31
TOP_K = 4
SWIGLU_LIMIT = 7.0
SWIGLU_ALPHA = 1.702
N_MOD = 6
RMS_EPS = 1e-6
LN_EPS = 1e-5
PAST_LEN = 16384

ROW_TILE = 128
MM_ROW_TILE = 640
MM_COL_TILE = 512
HALO_A = 16
HALO_C = 32
CONV_ROW_BLOCK = 64
LN_ROW_BLOCK = 16
MOE_CHUNK = 128
MOE_BIG = 4
MOE_SUPER = 12 * MOE_CHUNK
MOE_C1 = 512
MOE_C2 = 512
GATHER_ROWS = 256


def _params(sem, vmem=V7X_VMEM_LIMIT_BYTES):
    return pltpu.CompilerParams(dimension_semantics=sem, vmem_limit_bytes=vmem)


def _sigmoid(x):
    return 1.0 / (1.0 + jnp.exp(-x))


def _ada_body(c_ref, w_ref, b_ref, o_ref):
    c = c_ref[...]
    ca = (c * _sigmoid(c)).astype(bf16)
    o_ref[...] = jnp.dot(ca, w_ref[...].astype(bf16), preferred_element_type=f32) + b_ref[...]


def _ada(c_all, w_ada, b_ada):
    depth, d, n = w_ada.shape
    rows = c_all.shape[0]
    tn = MM_COL_TILE
    return pl.pallas_call(
        _ada_body,
        grid=(depth, n // tn),
        in_specs=[
            pl.BlockSpec((rows, d), lambda l, j: (0, 0)),
            pl.BlockSpec((None, d, tn), lambda l, j: (l, 0, j)),
            pl.BlockSpec((None, 1, tn), lambda l, j: (l, 0, j)),
        ],
        out_specs=pl.BlockSpec((None, rows, tn), lambda l, j: (l, 0, j)),
        out_shape=jax.ShapeDtypeStruct((depth, rows, n), f32),
        compiler_params=_params(("arbitrary", "arbitrary")),
        name="ada",
    )(c_all, w_ada, b_ada.reshape(depth, 1, n))


def _rms(x, g):
    return x * lax.rsqrt(jnp.mean(x * x, axis=-1, keepdims=True) + RMS_EPS) * g


def _rowwise_body(*refs, n_prompt_tiles, has_resid, has_norm, has_router, split_in, split_out):
    refs = list(refs)
    is_sample = pl.program_id(0) == n_prompt_tiles

    def mod(ref_s, ref_p):
        return jnp.where(is_sample, ref_s[...], ref_p[0])

    if split_in:
        xp_ref, xs_ref = refs.pop(0), refs.pop(0)
        x = jnp.where(is_sample, xs_ref[...], xp_ref[...])
    else:
        x = refs.pop(0)[...]
    if has_resid:
        y_ref, gy_ref, gate_s, gate_p = refs[:4]
        refs = refs[4:]
        x = x + mod(gate_s, gate_p) * _rms(y_ref[...], gy_ref[...])
    if has_norm:
        gh_ref, scale_s, scale_p, shift_s, shift_p = refs[:5]
        refs = refs[5:]
        h = _rms(x, gh_ref[...]) * (1.0 + mod(scale_s, scale_p)) + mod(shift_s, shift_p)
    if has_router:
        wr_ref, br_ref = refs[:2]
        refs = refs[2:]
    outs = refs
    if has_resid and split_out:
        xo_p, xo_s = outs.pop(0), outs.pop(0)

        @pl.when(is_sample)
        def _():
            xo_s[...] = x

        @pl.when(jnp.logical_not(is_sample))
        def _():
            xo_p[...] = x
    elif has_resid:
        outs.pop(0)[...] = x
    if has_norm:
        h_ref = outs.pop(0)
        h_ref[...] = h.astype(h_ref.dtype)
    if has_router:
        idx_ref, gate_ref = outs
        logits = jnp.dot(h, wr_ref[...], preferred_element_type=f32, precision=lax.Precision.HIGHEST) + br_ref[...]
        n_exp = logits.shape[-1]
        lane = lax.broadcasted_iota(i32, logits.shape, 1)
        vals, idxs = [], []
        for _ in range(TOP_K):
            m = jnp.max(logits, axis=-1, keepdims=True)
            idx = jnp.min(jnp.where(logits == m, lane, n_exp), axis=-1, keepdims=True)
            vals.append(m)
            idxs.append(idx)
            logits = jnp.where(lane == idx, -jnp.inf, logits)
        es = [jnp.exp(v - vals[0]) for v in vals]
        denom = es[0] + es[1] + es[2] + es[3]
        out_lane = lax.broadcasted_iota(i32, idx_ref.shape, 1)
        idx_out = jnp.zeros(idx_ref.shape, i32)
        gate_out = jnp.zeros(gate_ref.shape, f32)
        for k in range(TOP_K):
            idx_out = jnp.where(out_lane == k, idxs[k], idx_out)
            gate_out = jnp.where(out_lane == k, es[k] / denom, gate_out)
        idx_ref[...] = idx_out
        gate_ref[...] = gate_out


def _rowwise(x, mod_s, mod_p, *, n_prompt_rows, rows_per_seq, resid=None, norm=None, router=None, h_dtype=bf16,
             norm_mods=None, split_out=False):
    tm = ROW_TILE
    n_prompt_tiles = n_prompt_rows // tm
    tiles_per_seq = rows_per_seq // tm
    n_batch = mod_p.shape[0]
    split_in = isinstance(x, tuple)
    d = x[0].shape[1] if split_in else x.shape[1]
    r = n_prompt_rows + tm
    row_spec = pl.BlockSpec((tm, d), lambda i: (i, 0))
    prompt_spec = pl.BlockSpec((tm, d), lambda i: (jnp.minimum(i, n_prompt_tiles - 1), 0))
    sample_spec = pl.BlockSpec((tm, d), lambda i: (0, 0))
    vec_spec = pl.BlockSpec((1, d), lambda i: (0, 0))

    def mod_specs(col):
        return [
            pl.BlockSpec((mod_s.shape[0], d), lambda i: (0, col)),
            pl.BlockSpec((1, 1, d), lambda i: (jnp.minimum(i // tiles_per_seq, n_batch - 1), 0, col)),
        ]

    out_shape, out_specs = [], []
    if split_in:
        args, in_specs = list(x), [prompt_spec, sample_spec]
    else:
        args, in_specs = [x], [row_spec]
    if resid is not None:
        y, gain_y, gate_col = resid
        args += [y, gain_y.reshape(1, d), mod_s, mod_p]
        in_specs += [row_spec, vec_spec] + mod_specs(gate_col)
        if split_out:
            out_shape += [jax.ShapeDtypeStruct((n_prompt_rows, d), f32), jax.ShapeDtypeStruct((tm, d), f32)]
            out_specs += [prompt_spec, sample_spec]
        else:
            out_shape.append(jax.ShapeDtypeStruct((r, d), f32))
            out_specs.append(row_spec)
    if norm is not None:
        gain_h, scale_col, shift_col = norm
        nm_s, nm_p = (mod_s, mod_p) if norm_mods is None else norm_mods
        args += [gain_h.reshape(1, d), nm_s, nm_p, nm_s, nm_p]
        in_specs += [vec_spec] + mod_specs(scale_col) + mod_specs(shift_col)
        out_shape.append(jax.ShapeDtypeStruct((r, d), h_dtype))
        out_specs.append(row_spec)
    if router is not None:
        w_router, b_router = router
        n_exp = w_router.shape[1]
        args += [w_router, b_router.reshape(1, n_exp)]
        in_specs += [pl.BlockSpec((d, n_exp), lambda i: (0, 0)), pl.BlockSpec((1, n_exp), lambda i: (0, 0))]
        out_shape += [jax.ShapeDtypeStruct((r, LANE), i32), jax.ShapeDtypeStruct((r, LANE), f32)]
        out_specs += [pl.BlockSpec((tm, LANE), lambda i: (i, 0))] * 2
    body = functools.partial(_rowwise_body, n_prompt_tiles=n_prompt_tiles, has_resid=resid is not None,
                             has_norm=norm is not None, has_router=router is not None,
                             split_in=split_in, split_out=split_out)
    return pl.pallas_call(
        body, grid=(r // tm,), in_specs=in_specs, out_specs=out_specs, out_shape=out_shape,
        compiler_params=_params(("arbitrary",)), name="rowwise",
    )(*args)


def _mm_body(*refs, has_bias):
    if has_bias:
        x_ref, w_ref, b_ref, o_ref, wb_ref = refs
    else:
        x_ref, w_ref, o_ref, wb_ref = refs

    @pl.when(pl.program_id(1) == 0)
    def _():
        wb_ref[...] = w_ref[...].astype(bf16)

    acc = jnp.dot(x_ref[...], wb_ref[...], preferred_element_type=f32)
    if has_bias:
        acc = acc + b_ref[...]
    o_ref[...] = acc


def _mm(x, w, bias=None):
    m, k = x.shape
    n = w.shape[1]
    tm, tn = MM_ROW_TILE, MM_COL_TILE
    args = [x, w]
    in_specs = [pl.BlockSpec((tm, k), lambda j, i: (i, 0)), pl.BlockSpec((k, tn), lambda j, i: (0, j))]
    if bias is not None:
        args.append(bias.reshape(1, n))
        in_specs.append(pl.BlockSpec((1, tn), lambda j, i: (0, j)))
    return pl.pallas_call(
        functools.partial(_mm_body, has_bias=bias is not None),
        grid=(n // tn, m // tm),
        in_specs=in_specs,
        out_specs=pl.BlockSpec((tm, tn), lambda j, i: (i, j)),
        out_shape=jax.ShapeDtypeStruct((m, n), f32),
        scratch_shapes=[pltpu.VMEM((k, tn), bf16)],
        compiler_params=_params(("arbitrary", "arbitrary")),
        name="mm",
    )(*args)


def _pool_mix(z, pw_ref, ps_ref, gi, c0, cw):
    yb = jnp.dot(z.astype(bf16), pw_ref[gi].astype(bf16), preferred_element_type=f32)
    return yb * ps_ref[:, c0:c0 + cw]


def _even_prompt_body(p_ref, ph_ref, ysm_ref, cw_ref, pw_ref, ps_ref, y_ref, sa_ref, sb_ref, eg_ref, eb_ref,
                      *, tl, da, tiles, n_tiles):
    step = pl.program_id(0)

    @pl.when(step == n_tiles)
    def _sample_rows():
        y_ref[...] = ysm_ref[...]

    @pl.when(step < n_tiles)
    def _prompt_rows():
        _even_prompt_tile(p_ref, ph_ref, cw_ref, pw_ref, ps_ref, y_ref, sa_ref, sb_ref, eg_ref, eb_ref,
                          tl=tl, da=da, tile_in_seq=step % tiles)


def _even_prompt_tile(p_ref, ph_ref, cw_ref, pw_ref, ps_ref, y_ref, sa_ref, sb_ref, eg_ref, eb_ref,
                      *, tl, da, tile_in_seq):
    first = tile_in_seq == 0
    a_pre, a_post = p_ref[:, 0:da], p_ref[:, da:2 * da]
    a_val, b_in = p_ref[:, 2 * da:3 * da], p_ref[:, 3 * da:4 * da]
    g = a_pre * a_val
    eg_ref[0:HALO_A, :] = jnp.where(first, 0.0, ph_ref[:, 0:da] * ph_ref[:, 2 * da:3 * da])
    eg_ref[HALO_A:, :] = g
    eb_ref[0:HALO_A, :] = jnp.where(first, 0.0, ph_ref[:, 3 * da:4 * da])
    eb_ref[HALO_A:, :] = b_in
    conv = (cw_ref[0:1, :] * eg_ref[HALO_A - 2:HALO_A - 2 + tl, :]
            + cw_ref[1:2, :] * eg_ref[HALO_A - 1:HALO_A - 1 + tl, :]
            + cw_ref[2:3, :] * g)
    y_ref[:, 0:da] = (a_post * conv).astype(y_ref.dtype)
    sa_ref[...] = eg_ref[HALO_A + tl - (CONV_A_K - 1):HALO_A + tl, :]
    sb_ref[...] = eb_ref[HALO_A + tl - POOL_BUF:HALO_A + tl, :]
    pos = tile_in_seq * tl + lax.broadcasted_iota(i32, (tl, 1), 0)
    cw = da // len(POOL_WINDOWS)
    for gi, w in enumerate(POOL_WINDOWS):
        c0 = gi * cw
        cur = b_in[:, c0:c0 + cw]
        s = cur
        for back in range(1, w):
            s = s + eb_ref[HALO_A - back:HALO_A - back + tl, c0:c0 + cw]
        count = jnp.minimum(pos + 1, w).astype(f32)
        z = s / count - cur
        y_ref[:, da + c0:da + c0 + cw] = _pool_mix(z, pw_ref, ps_ref, gi, c0, cw).astype(y_ref.dtype)


def _even_prompt(proj, y_sample, conv_w, pool_w, pool_scale, *, n_batch, seq):
    d_in = proj.shape[1]
    da = d_in // 4
    tl = ROW_TILE
    tiles = seq // tl
    n_tiles = n_batch * tiles
    halo_per_tile = tl // HALO_A
    ng, cg = pool_w.shape[0], pool_w.shape[1]
    assert y_sample.shape == (tl, 2 * da)

    def tile(s):
        return jnp.minimum(s, n_tiles - 1)

    def batch(s):
        return tile(s) // tiles

    return pl.pallas_call(
        functools.partial(_even_prompt_body, tl=tl, da=da, tiles=tiles, n_tiles=n_tiles),
        grid=(n_tiles + 1,),
        in_specs=[
            pl.BlockSpec((tl, d_in), lambda s: (tile(s), 0)),
            pl.BlockSpec((HALO_A, d_in), lambda s: (jnp.maximum(tile(s) * halo_per_tile - 1, 0), 0)),
            pl.BlockSpec((tl, 2 * da), lambda s: (0, 0)),
            pl.BlockSpec((CONV_A_K, da), lambda s: (0, 0)),
            pl.BlockSpec((ng, cg, cg), lambda s: (0, 0, 0)),
            pl.BlockSpec((1, da), lambda s: (0, 0)),
        ],
        out_specs=[
            pl.BlockSpec((tl, 2 * da), lambda s: (s, 0)),
            pl.BlockSpec((None, CONV_A_K - 1, da), lambda s: (batch(s), 0, 0)),
            pl.BlockSpec((None, POOL_BUF, da), lambda s: (batch(s), 0, 0)),
        ],
        out_shape=[
            jax.ShapeDtypeStruct(((n_tiles + 1) * tl, 2 * da), bf16),
            jax.ShapeDtypeStruct((n_batch, CONV_A_K - 1, da), f32),
            jax.ShapeDtypeStruct((n_batch, POOL_BUF, da), f32),
        ],
        scratch_shapes=[pltpu.VMEM((HALO_A + tl, da), f32), pltpu.VMEM((HALO_A + tl, da), f32)],
        compiler_params=_params(("arbitrary",)),
        name="even_prompt",
    )(proj, proj, y_sample, conv_w, pool_w, pool_scale.reshape(1, da))


def _even_sample_body(apre_ref, apost_ref, aval_ref, bin_ref, sa_ref, sb_ref, cw_ref, pw_ref, ps_ref,
                      y_ref, g_ref, *, ng):
    s_id = pl.program_id(0)

    @pl.when(s_id < ng)
    def _conv():
        g = apre_ref[...] * aval_ref[...]
        g_ref[...] = g
        conv = cw_ref[0:1, :] * sa_ref[0] + cw_ref[1:2, :] * sa_ref[1] + cw_ref[2:3, :] * g
        y_ref[...] = (apost_ref[...] * conv).astype(y_ref.dtype)

    @pl.when(s_id >= ng)
    def _pool():
        cur = bin_ref[...]
        s = cur
        z = jnp.zeros_like(cur)
        back = 1
        for gi, w in enumerate(POOL_WINDOWS):
            while back < w:
                s = s + sb_ref[POOL_BUF - back]
                back += 1
            count = float(min(PAST_LEN + 1, w))
            z = jnp.where(s_id - ng == gi, s / count - cur, z)
        yb = jnp.dot(z.astype(bf16), pw_ref[...].astype(bf16), preferred_element_type=f32) * ps_ref[...]
        y_ref[...] = yb.astype(y_ref.dtype)


def _even_sample(proj, state_a_t, state_b_t, conv_w, pool_w, pool_scale, *, row_block):
    da = proj.shape[1] // 4
    ng, cg = pool_w.shape[0], pool_w.shape[1]
    nb = state_a_t.shape[1]
    assert nb == ROW_TILE and cg * ng == da and ng == len(POOL_WINDOWS)

    def conv_chunk(s):
        return jnp.minimum(s, ng - 1)

    def pool_group(s):
        return jnp.maximum(s - ng, 0)

    def col(k):
        return pl.BlockSpec((nb, cg), lambda s: (row_block, k * ng + conv_chunk(s)))

    return pl.pallas_call(
        functools.partial(_even_sample_body, ng=ng),
        grid=(2 * ng,),
        in_specs=[
            col(0), col(1), col(2),
            pl.BlockSpec((nb, cg), lambda s: (row_block, 3 * ng + pool_group(s))),
            pl.BlockSpec((CONV_A_K - 1, nb, cg), lambda s: (0, 0, conv_chunk(s))),
            pl.BlockSpec((POOL_BUF, nb, cg), lambda s: (0, 0, pool_group(s))),
            pl.BlockSpec((CONV_A_K, cg), lambda s: (0, conv_chunk(s))),
            pl.BlockSpec((None, cg, cg), lambda s: (pool_group(s), 0, 0)),
            pl.BlockSpec((1, cg), lambda s: (0, pool_group(s))),
        ],
        out_specs=[
            pl.BlockSpec((nb, cg), lambda s: (0, s)),
            pl.BlockSpec((nb, cg), lambda s: (0, conv_chunk(s))),
        ],
        out_shape=[
            jax.ShapeDtypeStruct((nb, 2 * da), bf16),
            jax.ShapeDtypeStruct((nb, da), f32),
        ],
        compiler_params=_params(("arbitrary",)),
        name="even_sample",
    )(proj, proj, proj, proj, state_a_t, state_b_t, conv_w, pool_w, pool_scale.reshape(1, da))


def _ln_swish(src_ref, dst_ref, g_ref, b_ref):
    rows = LN_ROW_BLOCK

    def block(i, carry):
        r = pl.ds(pl.multiple_of(i * rows, rows), rows)
        mu = jnp.mean(src_ref[r, :], axis=-1, keepdims=True)
        zc = src_ref[r, :] - mu
        var = jnp.mean(zc * zc, axis=-1, keepdims=True)
        y = (src_ref[r, :] - mu) * lax.rsqrt(var + LN_EPS) * g_ref[...] + b_ref[...]
        dst_ref[r, :] = (y * _sigmoid(y)).astype(dst_ref.dtype)
        return carry

    lax.fori_loop(0, src_ref.shape[0] // rows, block, 0)


def _odd_prompt_body(u_ref, uh_ref, zsm_ref, dw_ref, db_ref, lg_ref, lb_ref, z_ref, sc_ref, ext_ref, acc_ref, sh_ref,
                     *, tl, dc, tiles, n_tiles):
    step = pl.program_id(0)

    @pl.when(step == n_tiles)
    def _sample_rows():
        _ln_swish(zsm_ref, z_ref, lg_ref, lb_ref)

    @pl.when(step < n_tiles)
    def _prompt_rows():
        _odd_prompt_tile(u_ref, uh_ref, dw_ref, db_ref, lg_ref, lb_ref, z_ref, sc_ref, ext_ref, acc_ref, sh_ref,
                         tl=tl, dc=dc, first=step % tiles == 0)


def _odd_prompt_tile(u_ref, uh_ref, dw_ref, db_ref, lg_ref, lb_ref, z_ref, sc_ref, ext_ref, acc_ref, sh_ref,
                     *, tl, dc, first):
    v = u_ref[:, 0:dc] * _sigmoid(u_ref[:, dc:2 * dc])
    vh = uh_ref[:, 0:dc] * _sigmoid(uh_ref[:, dc:2 * dc])
    ext_ref[0:HALO_C, :] = jnp.where(first, 0.0, vh)
    ext_ref[HALO_C:, :] = v
    off = HALO_C - (CONV_C_K - 1)
    rb, a_max = CONV_ROW_BLOCK, (off + CONV_C_K - 1) // 8
    sh_rows = sh_ref.shape[1]
    for b in range(1, 8):
        sh_ref[b - 1] = ext_ref[b:b + sh_rows, :]
    for r0 in range(0, tl, rb):
        for c0 in range(0, dc, LANE):
            lanes = slice(c0, c0 + LANE)
            acc = jnp.broadcast_to(db_ref[:, lanes], (rb, LANE))
            for b in range(8):
                for a in range(a_max + 1):
                    k = 8 * a + b - off
                    if 0 <= k < CONV_C_K:
                        rows = slice(r0 + 8 * a, r0 + 8 * a + rb)
                        win = ext_ref[rows, lanes] if b == 0 else sh_ref[b - 1, rows, lanes]
                        acc = acc + dw_ref[k:k + 1, lanes] * win
            acc_ref[r0:r0 + rb, lanes] = acc
    _ln_swish(acc_ref, z_ref, lg_ref, lb_ref)
    sc_ref[...] = ext_ref[HALO_C + tl - (CONV_C_K - 1):HALO_C + tl, :]


def _odd_prompt(u, conv_sample, dw_w, dw_b, ln_g, ln_b, *, n_batch, seq):
    dc = u.shape[1] // 2
    tl = ROW_TILE
    tiles = seq // tl
    n_tiles = n_batch * tiles
    halo_per_tile = tl // HALO_C
    assert conv_sample.shape == (tl, dc)
    vec = pl.BlockSpec((1, dc), lambda s: (0, 0))

    def tile(s):
        return jnp.minimum(s, n_tiles - 1)

    return pl.pallas_call(
        functools.partial(_odd_prompt_body, tl=tl, dc=dc, tiles=tiles, n_tiles=n_tiles),
        grid=(n_tiles + 1,),
        in_specs=[
            pl.BlockSpec((tl, 2 * dc), lambda s: (tile(s), 0)),
            pl.BlockSpec((HALO_C, 2 * dc), lambda s: (jnp.maximum(tile(s) * halo_per_tile - 1, 0), 0)),
            pl.BlockSpec((tl, dc), lambda s: (0, 0)),
            pl.BlockSpec((CONV_C_K, dc), lambda s: (0, 0)),
            vec, vec, vec,
        ],
        out_specs=[
            pl.BlockSpec((tl, dc), lambda s: (s, 0)),
            pl.BlockSpec((None, CONV_C_K - 1, dc), lambda s: (tile(s) // tiles, 0, 0)),
        ],
        out_shape=[
            jax.ShapeDtypeStruct(((n_tiles + 1) * tl, dc), bf16),
            jax.ShapeDtypeStruct((n_batch, CONV_C_K - 1, dc), f32),
        ],
        scratch_shapes=[pltpu.VMEM((HALO_C + tl, dc), f32), pltpu.VMEM((tl, dc), f32),
                        pltpu.VMEM((7, tl + HALO_C - 8, dc), f32)],
        compiler_params=_params(("arbitrary",)),
        name="odd_prompt",
    )(u, u, conv_sample, dw_w, dw_b.reshape(1, dc), ln_g.reshape(1, dc), ln_b.reshape(1, dc))


def _odd_sample_conv_body(val_ref, gate_ref, sc_ref, dw_ref, db_ref, v_ref, a_ref):
    v = val_ref[...] * _sigmoid(gate_ref[...])
    v_ref[...] = v
    acc = dw_ref[CONV_C_K - 1:CONV_C_K, :] * v + db_ref[...]
    for k in range(CONV_C_K - 1):
        acc = acc + dw_ref[k:k + 1, :] * sc_ref[k]
    a_ref[...] = acc


def _odd_sample(u, state_c_t, dw_w, dw_b, *, row_block):
    dc = u.shape[1] // 2
    nb = state_c_t.shape[1]
    cc = MM_COL_TILE
    nchunk = dc // cc
    return pl.pallas_call(
        _odd_sample_conv_body,
        grid=(nchunk,),
        in_specs=[
            pl.BlockSpec((nb, cc), lambda j: (row_block, j)),
            pl.BlockSpec((nb, cc), lambda j: (row_block, nchunk + j)),
            pl.BlockSpec((CONV_C_K - 1, nb, cc), lambda j: (0, 0, j)),
            pl.BlockSpec((CONV_C_K, cc), lambda j: (0, j)),
            pl.BlockSpec((1, cc), lambda j: (0, j)),
        ],
        out_specs=[pl.BlockSpec((nb, cc), lambda j: (0, j))] * 2,
        out_shape=[jax.ShapeDtypeStruct((nb, dc), f32)] * 2,
        compiler_params=_params(("arbitrary",)),
        name="odd_sample_conv",
    )(u, u, state_c_t, dw_w, dw_b.reshape(1, dc))


def _gather_body(src_ref, h_hbm, o_ref, buf_ref, sem):
    i, n = pl.program_id(0), pl.num_programs(0)

    def row_copy(blk, r, tok):
        slot = lax.rem(blk, 2)
        return pltpu.make_async_copy(h_hbm.at[pl.ds(tok, 1), :], buf_ref.at[slot, pl.ds(r, 1), :], sem.at[slot])

    def for_valid_rows(blk, fn):
        def body(r, carry):
            tok = src_ref[blk * GATHER_ROWS + r]

            @pl.when(tok >= 0)
            def _():
                fn(row_copy(blk, r, tok))
            return carry
        lax.fori_loop(0, GATHER_ROWS, body, 0, unroll=8)

    @pl.when(i == 0)
    def _():
        buf_ref[...] = jnp.zeros(buf_ref.shape, buf_ref.dtype)
        for_valid_rows(0, lambda cp: cp.start())

    @pl.when(i + 1 < n)
    def _():
        for_valid_rows(i + 1, lambda cp: cp.start())

    for_valid_rows(i, lambda cp: cp.wait())
    o_ref[...] = buf_ref[lax.rem(i, 2)].astype(o_ref.dtype)


def _moe_gather(h, src_tok):
    p_total = src_tok.shape[0]
    d = h.shape[1]
    return pl.pallas_call(
        _gather_body,
        grid_spec=pltpu.PrefetchScalarGridSpec(
            num_scalar_prefetch=1,
            grid=(p_total // GATHER_ROWS,),
            in_specs=[pl.BlockSpec(memory_space=pl.ANY)],
            out_specs=pl.BlockSpec((GATHER_ROWS, d), lambda i, src: (i, 0)),
            scratch_shapes=[pltpu.VMEM((2, GATHER_ROWS, d), f32), pltpu.SemaphoreType.DMA((2,))],
        ),
        out_shape=jax.ShapeDtypeStruct((p_total, d), bf16),
        compiler_params=_params(("arbitrary",)),
        name="moe_gather",
    )(src_tok, h)


def _pair_act(h):
    n = h.shape[1]
    glu = jnp.minimum(h, SWIGLU_LIMIT)
    lin = jnp.clip(h, -SWIGLU_LIMIT, SWIGLU_LIMIT) + 1.0
    lin_next = pltpu.roll(lin, n - 1, axis=1)
    return glu * _sigmoid(SWIGLU_ALPHA * glu) * lin_next


def _expert_body(se_ref, sblk_ref, snc_ref, tail_ref, x_ref, w1_ref, b1_ref, wd_ref, bd_ref, ys_hbm,
                 w1b_ref, wdb_ref, act_ref, perm_ref, obuf_ref, osem, *, j1, k2):
    del se_ref
    i, j = pl.program_id(0), pl.program_id(1)
    n_chunks = snc_ref[i]
    half = MOE_C1 // 2
    ch = MOE_CHUNK
    row0 = sblk_ref[i] * ch

    n_big = n_chunks // MOE_BIG
    big = MOE_BIG * ch

    def for_row_blocks(fn):
        def big_body(b, carry):
            fn(pl.multiple_of(b * big, big), big)
            return carry

        def small_body(c, carry):
            fn(pl.multiple_of(c * ch, ch), ch)
            return carry

        lax.fori_loop(0, n_big, big_body, 0)
        lax.fori_loop(n_big * MOE_BIG, n_chunks, small_body, 0)

    def out_copy(src_slot, src_row, rows, dst_row, col_tile):
        return pltpu.make_async_copy(
            obuf_ref.at[src_slot, pl.ds(src_row, rows), :],
            ys_hbm.at[pl.ds(pl.multiple_of(dst_row, ch), rows), pl.ds(pl.multiple_of(col_tile * MOE_C2, MOE_C2), MOE_C2)],
            osem.at[src_slot])

    def wait_tile(slot, col_tile):
        for_row_blocks(lambda r0, rows: out_copy(slot, r0, rows, row0 + r0, col_tile).wait())

    @pl.when(j < j1)
    def _gate_up():
        w1b_ref[...] = w1_ref[...].astype(bf16)

        def rows_fn(r0, rows):
            even = (lax.broadcasted_iota(i32, (rows, half), 1) & 1) == 0
            hh = jnp.dot(x_ref[pl.ds(r0, rows), :], w1b_ref[...], preferred_element_type=f32) + b1_ref[...]
            pa = _pair_act(hh[:, :half])
            pb = _pair_act(hh[:, half:])
            act_ref[j, pl.ds(r0, rows), :] = jnp.where(even, pa, pltpu.roll(pb, 1, axis=1)).astype(bf16)

        for_row_blocks(rows_fn)

    @pl.when(j >= j1)
    def _down():
        k = j - j1
        slot = lax.rem(k, 2)

        @pl.when(k >= 2)
        def _():
            wait_tile(slot, k - 2)

        for jj in range(j1):
            for c in range(MOE_C2 // LANE):
                rows = slice(jj * half, (jj + 1) * half)
                cols = slice(c * LANE, (c + 1) * LANE)
                perm_ref[c, pl.ds(0, half // 2, stride=2), :] = wd_ref[jj * half:jj * half + half // 2, cols]
                perm_ref[c, pl.ds(1, half // 2, stride=2), :] = wd_ref[jj * half + half // 2:(jj + 1) * half, cols]
                wdb_ref[rows, cols] = perm_ref[c].astype(bf16)

        def rows_fn(r0, rows):
            a = jnp.concatenate([act_ref[jj, pl.ds(r0, rows), :] for jj in range(j1)], axis=1)
            obuf_ref[slot, pl.ds(r0, rows), :] = jnp.dot(a, wdb_ref[...], preferred_element_type=f32) + bd_ref[...]
            out_copy(slot, r0, rows, row0 + r0, k).start()

        for_row_blocks(rows_fn)

        @pl.when(k == k2 - 1)
        def _drain():
            wait_tile(1 - slot, k - 1)
            wait_tile(slot, k)

            @pl.when(i == pl.num_programs(0) - 1)
            def _tail():
                obuf_ref[0, 0:ch, :] = jnp.zeros((ch, MOE_C2), f32)
                tail_row0, n_tail = tail_ref[0] * ch, tail_ref[1]

                def start(t, carry):
                    for kk in range(k2):
                        out_copy(0, 0, ch, tail_row0 + t * ch, kk).start()
                    return carry

                def wait(t, carry):
                    for kk in range(k2):
                        out_copy(0, 0, ch, tail_row0 + t * ch, kk).wait()
                    return carry

                lax.fori_loop(0, n_tail, start, 0)
                lax.fori_loop(0, n_tail, wait, 0)


def _moe_experts(xs, w_gate_up, b_gate_up, w_down, b_down, layer, ys_rows, n_super, sup_e, sup_blk, sup_nc, tail):
    d = xs.shape[1]
    de = w_down.shape[2]
    j1 = (2 * de) // MOE_C1
    k2 = d // MOE_C2
    assert k2 >= 2
    s = MOE_SUPER
    el = pl.Element

    def w1_idx(i, j, se, sb, nc, tl):
        return (layer, se[i], 0, jnp.minimum(j, j1 - 1))

    def wd_idx(i, j, se, sb, nc, tl):
        return (layer, se[i], 0, jnp.maximum(j - j1, 0))

    return pl.pallas_call(
        functools.partial(_expert_body, j1=j1, k2=k2),
        grid_spec=pltpu.PrefetchScalarGridSpec(
            num_scalar_prefetch=4,
            grid=(n_super, j1 + k2),
            in_specs=[
                pl.BlockSpec((el(s), el(d)), lambda i, j, se, sb, nc, tl: (sb[i] * MOE_CHUNK, 0),
                             pipeline_mode=pl.Buffered(1)),
                pl.BlockSpec((None, None, d, MOE_C1), w1_idx),
                pl.BlockSpec((None, None, 1, MOE_C1), w1_idx),
                pl.BlockSpec((None, None, de, MOE_C2), wd_idx),
                pl.BlockSpec((None, None, 1, MOE_C2), wd_idx),
            ],
            out_specs=pl.BlockSpec(memory_space=pl.ANY),
            scratch_shapes=[
                pltpu.VMEM((d, MOE_C1), bf16),
                pltpu.VMEM((de, MOE_C2), bf16),
                pltpu.VMEM((j1, s, MOE_C1 // 2), bf16),
                pltpu.VMEM((MOE_C2 // LANE, MOE_C1 // 2, LANE), f32),
                pltpu.VMEM((2, s, MOE_C2), f32),
                pltpu.SemaphoreType.DMA((2,)),
            ],
        ),
        out_shape=jax.ShapeDtypeStruct((ys_rows, d), f32),
        compiler_params=_params(("arbitrary", "arbitrary")),
        name="moe_experts",
    )(sup_e, sup_blk, sup_nc, tail, xs, w_gate_up, b_gate_up.reshape(*b_gate_up.shape[:2], 1, -1),
      w_down, b_down.reshape(*b_down.shape[:2], 1, -1))


def _combine_body(dest_ref, ys_hbm, gate_ref, o_ref, buf_ref, sem):
    tm = o_ref.shape[0]
    i, n = pl.program_id(0), pl.num_programs(0)

    def row_copy(tile, r, k):
        slot = lax.rem(tile, 2)
        p = dest_ref[(tile * tm + r) * TOP_K + k]
        return pltpu.make_async_copy(ys_hbm.at[pl.ds(p, 1), :], buf_ref.at[slot, k, pl.ds(r, 1), :], sem.at[slot])

    def start_tile(tile):
        def body(r, carry):
            for k in range(TOP_K):
                row_copy(tile, r, k).start()
            return carry
        lax.fori_loop(0, tm, body, 0, unroll=4)

    @pl.when(i == 0)
    def _():
        start_tile(0)

    @pl.when(i + 1 < n)
    def _():
        start_tile(i + 1)

    def wait(r, carry):
        for k in range(TOP_K):
            row_copy(i, r, k).wait()
        return carry

    lax.fori_loop(0, tm, wait, 0, unroll=4)
    slot = lax.rem(i, 2)
    g = gate_ref[...]
    acc = buf_ref[slot, 0] * g[:, 0:1]
    for k in range(1, TOP_K):
        acc = acc + buf_ref[slot, k] * g[:, k:k + 1]
    o_ref[...] = acc


def _moe_combine(ys, dest, gates, n_rows):
    d = ys.shape[1]
    tm = ROW_TILE
    return pl.pallas_call(
        _combine_body,
        grid_spec=pltpu.PrefetchScalarGridSpec(
            num_scalar_prefetch=1,
            grid=(n_rows // tm,),
            in_specs=[pl.BlockSpec(memory_space=pl.ANY), pl.BlockSpec((tm, LANE), lambda i, dest: (i, 0))],
            out_specs=pl.BlockSpec((tm, d), lambda i, dest: (i, 0)),
            scratch_shapes=[pltpu.VMEM((2, TOP_K, tm, d), f32), pltpu.SemaphoreType.DMA((2,))],
        ),
        out_shape=jax.ShapeDtypeStruct((n_rows, d), f32),
        compiler_params=_params(("arbitrary",)),
        name="moe_combine",
    )(dest, ys, gates)


def _moe_plan(top_idx, n_exp):
    t = top_idx.shape[0]
    n_assign = t * TOP_K
    flat_e = top_idx.reshape(-1)
    order = jnp.argsort(flat_e).astype(i32)
    rank = jnp.argsort(order).astype(i32)
    counts = jnp.sum((flat_e[:, None] == jnp.arange(n_exp, dtype=i32)[None, :]).astype(i32), axis=0)
    padded = (counts + MOE_CHUNK - 1) // MOE_CHUNK * MOE_CHUNK
    start = jnp.cumsum(counts) - counts
    pstart = jnp.cumsum(padded) - padded
    dest = (pstart[flat_e] + rank - start[flat_e]).astype(i32)
    max_rows = -(-(n_assign + n_exp * (MOE_CHUNK - 1)) // MOE_CHUNK) * MOE_CHUNK
    p_total = -(-(max_rows + MOE_SUPER) // GATHER_ROWS) * GATHER_ROWS
    p = jnp.arange(p_total, dtype=i32)
    e_of_p = jnp.minimum(jnp.sum((p[:, None] >= (pstart + padded)[None, :]).astype(i32), axis=1), n_exp - 1)
    in_group = p - pstart[e_of_p]
    src_assign = order[jnp.clip(start[e_of_p] + in_group, 0, n_assign - 1)]
    src_tok = jnp.where(in_group < counts[e_of_p], src_assign // TOP_K, -1).astype(i32)
    n_sup_e = (padded + MOE_SUPER - 1) // MOE_SUPER
    cum = jnp.cumsum(n_sup_e)
    max_super = n_exp + max_rows // MOE_SUPER
    sid = jnp.arange(max_super, dtype=i32)
    sup_e = jnp.minimum(jnp.searchsorted(cum, sid, side="right"), n_exp - 1).astype(i32)
    within = sid - (cum[sup_e] - n_sup_e[sup_e])
    row0 = pstart[sup_e] + within * MOE_SUPER
    rows = jnp.clip(padded[sup_e] - within * MOE_SUPER, 0, MOE_SUPER)
    rows = jnp.where(sid < cum[-1], rows, 0)
    used = jnp.sum(padded)
    tail = jnp.stack([used // MOE_CHUNK, (max_rows - used) // MOE_CHUNK]).astype(i32)
    return dict(dest=dest, src_tok=src_tok, n_super=cum[-1].astype(i32), sup_e=sup_e, ys_rows=max_rows,
                sup_blk=(row0 // MOE_CHUNK).astype(i32), sup_nc=(rows // MOE_CHUNK).astype(i32), tail=tail)


def _moe(h, top_idx, gates, w_gate_up, b_gate_up, w_down, b_down, layer):
    plan = _moe_plan(top_idx[:, :TOP_K], w_down.shape[1])
    xs = _moe_gather(h, plan["src_tok"])
    ys = _moe_experts(xs, w_gate_up, b_gate_up, w_down, b_down, layer, plan["ys_rows"],
                      plan["n_super"], plan["sup_e"], plan["sup_blk"], plan["sup_nc"], plan["tail"])
    return _moe_combine(ys, plan["dest"], gates, h.shape[0])


def kernel(x_prompt, x_sample, c_prompt, c_sample, state_conv_a, state_pool_b, state_conv_c, w_ada, b_ada, norm_gains, ab_w_in, a_conv_w, b_pool_w, b_pool_scale, ab_w_out, c_pw1_w, c_pw1_b, c_dw_w, c_dw_b, c_ln_g, c_ln_b, c_pw2_w, c_pw2_b, moe_w_router, moe_b_router, moe_w_gate_up, moe_b_gate_up, moe_w_down, moe_b_down):
    n_batch, seq, d = x_prompt.shape
    n_dec, dec_seq, _ = x_sample.shape
    depth = w_ada.shape[0]
    assert dec_seq == 1 and n_dec == ROW_TILE and seq % ROW_TILE == 0
    n_prompt_rows = n_batch * seq
    r = n_prompt_rows + n_dec
    assert r % MM_ROW_TILE == 0
    sample_block = n_prompt_rows // ROW_TILE

    x = (x_prompt.reshape(n_prompt_rows, d), x_sample.reshape(n_dec, d))
    pad = (-(n_dec + n_batch)) % 8
    c_all = jnp.concatenate([c_sample, c_prompt, jnp.zeros((pad, d), f32)], axis=0)
    mod_all = _ada(c_all, w_ada, b_ada)
    rowwise = functools.partial(_rowwise, n_prompt_rows=n_prompt_rows, rows_per_seq=seq)

    new_a_p, new_b_p, new_c_p, new_a_s, new_b_s, new_c_s = [], [], [], [], [], []
    h = None
    for l in range(depth):
        mod_s = mod_all[l, :n_dec]
        mod_p = mod_all[l, n_dec:n_dec + n_batch].reshape(n_batch, 1, N_MOD * d)
        ng = norm_gains[l]
        if l == 0:
            h = rowwise(x, mod_s, mod_p, norm=(ng[0], 1, 0))[0]
        if l % 2 == 0:
            e = l // 2
            proj = _mm(h, ab_w_in[e])
            y_s, g_s = _even_sample(proj, state_conv_a[e].transpose(1, 0, 2), state_pool_b[e].transpose(1, 0, 2),
                                    a_conv_w[e], b_pool_w[e], b_pool_scale[e], row_block=sample_block)
            ycat, sa, sb = _even_prompt(proj, y_s, a_conv_w[e], b_pool_w[e], b_pool_scale[e],
                                        n_batch=n_batch, seq=seq)
            y = _mm(ycat, ab_w_out[e])
            new_a_p.append(sa)
            new_b_p.append(sb)
            new_a_s.append(jnp.concatenate([state_conv_a[e][:, 1:], g_s[:, None]], axis=1))
            b_in_s = proj[n_prompt_rows:, 3 * (d // 2):]
            new_b_s.append(jnp.concatenate([state_pool_b[e][:, 1:], b_in_s[:, None]], axis=1))
        else:
            o = l // 2
            u = _mm(h, c_pw1_w[o], c_pw1_b[o])
            v_s, conv_s = _odd_sample(u, state_conv_c[o].transpose(1, 0, 2), c_dw_w[o], c_dw_b[o],
                                      row_block=sample_block)
            zcat, s_c = _odd_prompt(u, conv_s, c_dw_w[o], c_dw_b[o], c_ln_g[o], c_ln_b[o],
                                    n_batch=n_batch, seq=seq)
            y = _mm(zcat, c_pw2_w[o], c_pw2_b[o])
            new_c_p.append(s_c)
            new_c_s.append(jnp.concatenate([state_conv_c[o][:, 1:], v_s[:, None]], axis=1))
        x, h2, top_idx, gates = rowwise(x, mod_s, mod_p, resid=(y, ng[1], 2), norm=(ng[2], 4, 3),
                                        router=(moe_w_router[l], moe_b_router[l]), h_dtype=f32)
        y = _moe(h2, top_idx, gates, moe_w_gate_up, moe_b_gate_up, moe_w_down, moe_b_down, l)
        if l + 1 < depth:
            mod_s_n = mod_all[l + 1, :n_dec]
            mod_p_n = mod_all[l + 1, n_dec:n_dec + n_batch].reshape(n_batch, 1, N_MOD * d)
            x, h = rowwise(x, mod_s, mod_p, resid=(y, ng[3], 5), norm=(norm_gains[l + 1][0], 1, 0),
                           norm_mods=(mod_s_n, mod_p_n))
        else:
            x = rowwise(x, mod_s, mod_p, resid=(y, ng[3], 5), split_out=True)

    y_prompt = x[0].reshape(n_batch, seq, d)
    y_sample = x[1].reshape(n_dec, 1, d)
    return (y_prompt, y_sample, jnp.stack(new_a_p), jnp.stack(new_b_p), jnp.stack(new_c_p),
            jnp.stack(new_a_s), jnp.stack(new_b_s), jnp.stack(new_c_s))
```

```python
import functools

import jax
import jax.numpy as jnp
from jax import lax
from jax.experimental import pallas as pl
from jax.experimental.pallas import tpu as pltpu

f32 = jnp.float32
bf16 = jnp.bfloat16
i32 = jnp.int32

LANE = 128
V7X_VMEM_LIMIT_BYTES = 56 * 1024 * 1024

POOL_WINDOWS = (2, 4, 8, 16)
POOL_BUF = max(POOL_WINDOWS) - 1
CONV_A_K = 3
CONV_C_K = 31
TOP_K = 4
SWIGLU_LIMIT = 7.0
SWIGLU_ALPHA = 1.702
N_MOD = 6
RMS_EPS = 1e-6
LN_EPS = 1e-5
PAST_LEN = 16384

ROW_TILE = 128
MM_ROW_TILE = 1664
MM_COL_TILE = 512
HALO_A = 16
HALO_C = 32
CONV_ROW_BLOCK = 64
LN_ROW_BLOCK = 16
MOE_CHUNK = 128
MOE_BLOCKS = (4, 1)
MOE_SUPER = 12 * MOE_CHUNK
MOE_C1 = 512
MOE_C2 = 512
GATHER_ROWS = 256
GATHER_STRIDE = 32


def _params(sem, vmem=V7X_VMEM_LIMIT_BYTES):
    return pltpu.CompilerParams(dimension_semantics=sem, vmem_limit_bytes=vmem)


def _sigmoid(x):
    return 1.0 / (1.0 + jnp.exp(-x))


def _ada_body(c_ref, w_ref, b_ref, o_ref):
    c = c_ref[...]
    ca = (c * _sigmoid(c)).astype(bf16)
    o_ref[...] = jnp.dot(ca, w_ref[...].astype(bf16), preferred_element_type=f32) + b_ref[...]


def _ada(c_all, w_ada, b_ada):
    depth, d, n = w_ada.shape
    rows = c_all.shape[0]
    tn = MM_COL_TILE
    return pl.pallas_call(
        _ada_body,
        grid=(depth, n // tn),
        in_specs=[
            pl.BlockSpec((rows, d), lambda l, j: (0, 0)),
            pl.BlockSpec((None, d, tn), lambda l, j: (l, 0, j)),
            pl.BlockSpec((None, 1, tn), lambda l, j: (l, 0, j)),
        ],
        out_specs=pl.BlockSpec((None, rows, tn), lambda l, j: (l, 0, j)),
        out_shape=jax.ShapeDtypeStruct((depth, rows, n), f32),
        compiler_params=_params(("arbitrary", "arbitrary")),
        name="ada",
    )(c_all, w_ada, b_ada.reshape(depth, 1, n))


def _rms(x, g):
    return x * lax.rsqrt(jnp.mean(x * x, axis=-1, keepdims=True) + RMS_EPS) * g


def _rowwise_body(*refs, n_prompt_tiles, has_resid, has_norm, has_router, split_in, split_out):
    refs = list(refs)
    is_sample = pl.program_id(0) == n_prompt_tiles

    def mod(ref_s, ref_p):
        return jnp.where(is_sample, ref_s[...], ref_p[0])

    if split_in:
        xp_ref, xs_ref = refs.pop(0), refs.pop(0)
        x = jnp.where(is_sample, xs_ref[...], xp_ref[...])
    else:
        x = refs.pop(0)[...]
    if has_resid:
        y_ref, gy_ref, gate_s, gate_p = refs[:4]
        refs = refs[4:]
        x = x + mod(gate_s, gate_p) * _rms(y_ref[...], gy_ref[...])
    if has_norm:
        gh_ref, scale_s, scale_p, shift_s, shift_p = refs[:5]
        refs = refs[5:]
        h = _rms(x, gh_ref[...]) * (1.0 + mod(scale_s, scale_p)) + mod(shift_s, shift_p)
    if has_router:
        wr_ref, br_ref = refs[:2]
        refs = refs[2:]
    outs = refs
    if has_resid and split_out:
        xo_p, xo_s = outs.pop(0), outs.pop(0)

        @pl.when(is_sample)
        def _():
            xo_s[...] = x

        @pl.when(jnp.logical_not(is_sample))
        def _():
            xo_p[...] = x
    elif has_resid:
        outs.pop(0)[...] = x
    if has_norm:
        h_ref = outs.pop(0)
        h_ref[...] = h.astype(h_ref.dtype)
    if has_router:
        idx_ref, gate_ref = outs
        wr = wr_ref[...]
        h_hi, wr_hi = h.astype(bf16), wr.astype(bf16)
        h_lo = (h - h_hi.astype(f32)).astype(bf16)
        wr_lo = (wr - wr_hi.astype(f32)).astype(bf16)
        logits = (jnp.dot(h_hi, wr_hi, preferred_element_type=f32) + jnp.dot(h_hi, wr_lo, preferred_element_type=f32)
                  + jnp.dot(h_lo, wr_hi, preferred_element_type=f32) + br_ref[...])
        n_exp = logits.shape[-1]
        lane = lax.broadcasted_iota(i32, logits.shape, 1)
        vals, idxs = [], []
        for _ in range(TOP_K):
            m = jnp.max(logits, axis=-1, keepdims=True)
            idx = jnp.min(jnp.where(logits == m, lane, n_exp), axis=-1, keepdims=True)
            vals.append(m)
            idxs.append(idx)
            logits = jnp.where(lane == idx, -jnp.inf, logits)
        es = [jnp.exp(v - vals[0]) for v in vals]
        denom = es[0] + es[1] + es[2] + es[3]
        out_lane = lax.broadcasted_iota(i32, idx_ref.shape, 1)
        idx_out = jnp.zeros(idx_ref.shape, i32)
        gate_out = jnp.zeros(gate_ref.shape, f32)
        for k in range(TOP_K):
            idx_out = jnp.where(out_lane == k, idxs[k], idx_out)
            gate_out = jnp.where(out_lane == k, es[k] / denom, gate_out)
        idx_ref[...] = idx_out
        gate_ref[...] = gate_out


def _rowwise(x, mod_s, mod_p, *, n_prompt_rows, rows_per_seq, resid=None, norm=None, router=None, h_dtype=bf16,
             norm_mods=None, split_out=False):
    tm = ROW_TILE
    n_prompt_tiles = n_prompt_rows // tm
    tiles_per_seq = rows_per_seq // tm
    n_batch = mod_p.shape[0]
    split_in = isinstance(x, tuple)
    d = x[0].shape[1] if split_in else x.shape[1]
    r = n_prompt_rows + tm
    row_spec = pl.BlockSpec((tm, d), lambda i: (i, 0))
    prompt_spec = pl.BlockSpec((tm, d), lambda i: (jnp.minimum(i, n_prompt_tiles - 1), 0))
    sample_spec = pl.BlockSpec((tm, d), lambda i: (0, 0))
    vec_spec = pl.BlockSpec((1, d), lambda i: (0, 0))

    def mod_specs(col):
        return [
            pl.BlockSpec((mod_s.shape[0], d), lambda i: (0, col)),
            pl.BlockSpec((1, 1, d), lambda i: (jnp.minimum(i // tiles_per_seq, n_batch - 1), 0, col)),
        ]

    out_shape, out_specs = [], []
    if split_in:
        args, in_specs = list(x), [prompt_spec, sample_spec]
    else:
        args, in_specs = [x], [row_spec]
    if resid is not None:
        y, gain_y, gate_col = resid
        args += [y, gain_y.reshape(1, d), mod_s, mod_p]
        in_specs += [row_spec, vec_spec] + mod_specs(gate_col)
        if split_out:
            out_shape += [jax.ShapeDtypeStruct((n_prompt_rows, d), f32), jax.ShapeDtypeStruct((tm, d), f32)]
            out_specs += [prompt_spec, sample_spec]
        else:
            out_shape.append(jax.ShapeDtypeStruct((r, d), f32))
            out_specs.append(row_spec)
    if norm is not None:
        gain_h, scale_col, shift_col = norm
        nm_s, nm_p = (mod_s, mod_p) if norm_mods is None else norm_mods
        args += [gain_h.reshape(1, d), nm_s, nm_p, nm_s, nm_p]
        in_specs += [vec_spec] + mod_specs(scale_col) + mod_specs(shift_col)
        out_shape.append(jax.ShapeDtypeStruct((r, d), h_dtype))
        out_specs.append(row_spec)
    if router is not None:
        w_router, b_router = router
        n_exp = w_router.shape[1]
        args += [w_router, b_router.reshape(1, n_exp)]
        in_specs += [pl.BlockSpec((d, n_exp), lambda i: (0, 0)), pl.BlockSpec((1, n_exp), lambda i: (0, 0))]
        out_shape += [jax.ShapeDtypeStruct((r, LANE), i32), jax.ShapeDtypeStruct((r, LANE), f32)]
        out_specs += [pl.BlockSpec((tm, LANE), lambda i: (i, 0))] * 2
    body = functools.partial(_rowwise_body, n_prompt_tiles=n_prompt_tiles, has_resid=resid is not None,
                             has_norm=norm is not None, has_router=router is not None,
                             split_in=split_in, split_out=split_out)
    return pl.pallas_call(
        body, grid=(r // tm,), in_specs=in_specs, out_specs=out_specs, out_shape=out_shape,
        compiler_params=_params(("arbitrary",)), name="rowwise",
    )(*args)


def _mm_body(*refs, has_bias):
    if has_bias:
        x_ref, w_ref, b_ref, o_ref, wb_ref = refs
    else:
        x_ref, w_ref, o_ref, wb_ref = refs

    @pl.when(pl.program_id(1) == 0)
    def _():
        wb_ref[...] = w_ref[...].astype(bf16)

    acc = jnp.dot(x_ref[...], wb_ref[...], preferred_element_type=f32)
    if has_bias:
        acc = acc + b_ref[...]
    o_ref[...] = acc


def _mm(x, w, bias=None):
    m, k = x.shape
    n = w.shape[1]
    tm, tn = MM_ROW_TILE, MM_COL_TILE
    args = [x, w]
    in_specs = [pl.BlockSpec((tm, k), lambda j, i: (i, 0)), pl.BlockSpec((k, tn), lambda j, i: (0, j))]
    if bias is not None:
        args.append(bias.reshape(1, n))
        in_specs.append(pl.BlockSpec((1, tn), lambda j, i: (0, j)))
    return pl.pallas_call(
        functools.partial(_mm_body, has_bias=bias is not None),
        grid=(n // tn, m // tm),
        in_specs=in_specs,
        out_specs=pl.BlockSpec((tm, tn), lambda j, i: (i, j)),
        out_shape=jax.ShapeDtypeStruct((m, n), f32),
        scratch_shapes=[pltpu.VMEM((k, tn), bf16)],
        compiler_params=_params(("arbitrary", "arbitrary")),
        name="mm",
    )(*args)


def _pool_mix(z, pw_ref, ps_ref, gi, c0, cw):
    yb = jnp.dot(z.astype(bf16), pw_ref[gi].astype(bf16), preferred_element_type=f32)
    return yb * ps_ref[:, c0:c0 + cw]


def _even_prompt_body(p_ref, ph_ref, ysm_ref, cw_ref, pw_ref, ps_ref, y_ref, sa_ref, sb_ref, eg_ref, eb_ref,
                      *, tl, da, tiles, n_tiles):
    step = pl.program_id(0)

    @pl.when(step == n_tiles)
    def _sample_rows():
        y_ref[...] = ysm_ref[...]

    @pl.when(step < n_tiles)
    def _prompt_rows():
        _even_prompt_tile(p_ref, ph_ref, cw_ref, pw_ref, ps_ref, y_ref, sa_ref, sb_ref, eg_ref, eb_ref,
                          tl=tl, da=da, tile_in_seq=step % tiles)


def _even_prompt_tile(p_ref, ph_ref, cw_ref, pw_ref, ps_ref, y_ref, sa_ref, sb_ref, eg_ref, eb_ref,
                      *, tl, da, tile_in_seq):
    first = tile_in_seq == 0
    a_pre, a_post = p_ref[:, 0:da], p_ref[:, da:2 * da]
    a_val, b_in = p_ref[:, 2 * da:3 * da], p_ref[:, 3 * da:4 * da]
    g = a_pre * a_val
    eg_ref[0:HALO_A, :] = jnp.where(first, 0.0, ph_ref[:, 0:da] * ph_ref[:, 2 * da:3 * da])
    eg_ref[HALO_A:, :] = g
    eb_ref[0:HALO_A, :] = jnp.where(first, 0.0, ph_ref[:, 3 * da:4 * da])
    eb_ref[HALO_A:, :] = b_in
    conv = (cw_ref[0:1, :] * eg_ref[HALO_A - 2:HALO_A - 2 + tl, :]
            + cw_ref[1:2, :] * eg_ref[HALO_A - 1:HALO_A - 1 + tl, :]
            + cw_ref[2:3, :] * g)
    y_ref[:, 0:da] = (a_post * conv).astype(y_ref.dtype)
    sa_ref[...] = eg_ref[HALO_A + tl - (CONV_A_K - 1):HALO_A + tl, :]
    sb_ref[...] = eb_ref[HALO_A + tl - POOL_BUF:HALO_A + tl, :]
    pos = tile_in_seq * tl + lax.broadcasted_iota(i32, (tl, 1), 0)
    cw = da // len(POOL_WINDOWS)
    for gi, w in enumerate(POOL_WINDOWS):
        c0 = gi * cw
        cur = b_in[:, c0:c0 + cw]
        s = cur
        for back in range(1, w):
            s = s + eb_ref[HALO_A - back:HALO_A - back + tl, c0:c0 + cw]
        count = jnp.minimum(pos + 1, w).astype(f32)
        z = s / count - cur
        y_ref[:, da + c0:da + c0 + cw] = _pool_mix(z, pw_ref, ps_ref, gi, c0, cw).astype(y_ref.dtype)


def _even_prompt(proj, y_sample, conv_w, pool_w, pool_scale, *, n_batch, seq):
    d_in = proj.shape[1]
    da = d_in // 4
    tl = ROW_TILE
    tiles = seq // tl
    n_tiles = n_batch * tiles
    halo_per_tile = tl // HALO_A
    ng, cg = pool_w.shape[0], pool_w.shape[1]
    assert y_sample.shape == (tl, 2 * da)

    def tile(s):
        return jnp.minimum(s, n_tiles - 1)

    def batch(s):
        return tile(s) // tiles

    return pl.pallas_call(
        functools.partial(_even_prompt_body, tl=tl, da=da, tiles=tiles, n_tiles=n_tiles),
        grid=(n_tiles + 1,),
        in_specs=[
            pl.BlockSpec((tl, d_in), lambda s: (tile(s), 0)),
            pl.BlockSpec((HALO_A, d_in), lambda s: (jnp.maximum(tile(s) * halo_per_tile - 1, 0), 0)),
            pl.BlockSpec((tl, 2 * da), lambda s: (0, 0)),
            pl.BlockSpec((CONV_A_K, da), lambda s: (0, 0)),
            pl.BlockSpec((ng, cg, cg), lambda s: (0, 0, 0)),
            pl.BlockSpec((1, da), lambda s: (0, 0)),
        ],
        out_specs=[
            pl.BlockSpec((tl, 2 * da), lambda s: (s, 0)),
            pl.BlockSpec((None, CONV_A_K - 1, da), lambda s: (batch(s), 0, 0)),
            pl.BlockSpec((None, POOL_BUF, da), lambda s: (batch(s), 0, 0)),
        ],
        out_shape=[
            jax.ShapeDtypeStruct(((n_tiles + 1) * tl, 2 * da), bf16),
            jax.ShapeDtypeStruct((n_batch, CONV_A_K - 1, da), f32),
            jax.ShapeDtypeStruct((n_batch, POOL_BUF, da), f32),
        ],
        scratch_shapes=[pltpu.VMEM((HALO_A + tl, da), f32), pltpu.VMEM((HALO_A + tl, da), f32)],
        compiler_params=_params(("arbitrary",)),
        name="even_prompt",
    )(proj, proj, y_sample, conv_w, pool_w, pool_scale.reshape(1, da))


def _even_sample_body(apre_ref, apost_ref, aval_ref, bin_ref, sa_ref, sb_ref, cw_ref, pw_ref, ps_ref,
                      y_ref, g_ref, *, ng):
    s_id = pl.program_id(0)

    @pl.when(s_id < ng)
    def _conv():
        g = apre_ref[...] * aval_ref[...]
        g_ref[...] = g
        conv = cw_ref[0:1, :] * sa_ref[0] + cw_ref[1:2, :] * sa_ref[1] + cw_ref[2:3, :] * g
        y_ref[...] = (apost_ref[...] * conv).astype(y_ref.dtype)

    @pl.when(s_id >= ng)
    def _pool():
        cur = bin_ref[...]
        s = cur
        z = jnp.zeros_like(cur)
        back = 1
        for gi, w in enumerate(POOL_WINDOWS):
            while back < w:
                s = s + sb_ref[POOL_BUF - back]
                back += 1
            count = float(min(PAST_LEN + 1, w))
            z = jnp.where(s_id - ng == gi, s / count - cur, z)
        yb = jnp.dot(z.astype(bf16), pw_ref[...].astype(bf16), preferred_element_type=f32) * ps_ref[...]
        y_ref[...] = yb.astype(y_ref.dtype)


def _even_sample(proj, state_a_t, state_b_t, conv_w, pool_w, pool_scale, *, row_block):
    da = proj.shape[1] // 4
    ng, cg = pool_w.shape[0], pool_w.shape[1]
    nb = state_a_t.shape[1]
    assert nb == ROW_TILE and cg * ng == da and ng == len(POOL_WINDOWS)

    def conv_chunk(s):
        return jnp.minimum(s, ng - 1)

    def pool_group(s):
        return jnp.maximum(s - ng, 0)

    def col(k):
        return pl.BlockSpec((nb, cg), lambda s: (row_block, k * ng + conv_chunk(s)))

    return pl.pallas_call(
        functools.partial(_even_sample_body, ng=ng),
        grid=(2 * ng,),
        in_specs=[
            col(0), col(1), col(2),
            pl.BlockSpec((nb, cg), lambda s: (row_block, 3 * ng + pool_group(s))),
            pl.BlockSpec((CONV_A_K - 1, nb, cg), lambda s: (0, 0, conv_chunk(s))),
            pl.BlockSpec((POOL_BUF, nb, cg), lambda s: (0, 0, pool_group(s))),
            pl.BlockSpec((CONV_A_K, cg), lambda s: (0, conv_chunk(s))),
            pl.BlockSpec((None, cg, cg), lambda s: (pool_group(s), 0, 0)),
            pl.BlockSpec((1, cg), lambda s: (0, pool_group(s))),
        ],
        out_specs=[
            pl.BlockSpec((nb, cg), lambda s: (0, s)),
            pl.BlockSpec((nb, cg), lambda s: (0, conv_chunk(s))),
        ],
        out_shape=[
            jax.ShapeDtypeStruct((nb, 2 * da), bf16),
            jax.ShapeDtypeStruct((nb, da), f32),
        ],
        compiler_params=_params(("arbitrary",)),
        name="even_sample",
    )(proj, proj, proj, proj, state_a_t, state_b_t, conv_w, pool_w, pool_scale.reshape(1, da))


def _ln_swish(src_ref, dst_ref, g_ref, b_ref):
    rows = LN_ROW_BLOCK

    def block(i, carry):
        r = pl.ds(pl.multiple_of(i * rows, rows), rows)
        mu = jnp.mean(src_ref[r, :], axis=-1, keepdims=True)
        zc = src_ref[r, :] - mu
        var = jnp.mean(zc * zc, axis=-1, keepdims=True)
        y = (src_ref[r, :] - mu) * lax.rsqrt(var + LN_EPS) * g_ref[...] + b_ref[...]
        dst_ref[r, :] = (y * _sigmoid(y)).astype(dst_ref.dtype)
        return carry

    lax.fori_loop(0, src_ref.shape[0] // rows, block, 0)


def _odd_prompt_body(u_ref, uh_ref, zsm_ref, dw_ref, db_ref, lg_ref, lb_ref, z_ref, sc_ref, ext_ref, acc_ref, sh_ref,
                     *, tl, dc, tiles, n_tiles):
    step = pl.program_id(0)

    @pl.when(step == n_tiles)
    def _sample_rows():
        _ln_swish(zsm_ref, z_ref, lg_ref, lb_ref)

    @pl.when(step < n_tiles)
    def _prompt_rows():
        _odd_prompt_tile(u_ref, uh_ref, dw_ref, db_ref, lg_ref, lb_ref, z_ref, sc_ref, ext_ref, acc_ref, sh_ref,
                         tl=tl, dc=dc, first=step % tiles == 0)


def _odd_prompt_tile(u_ref, uh_ref, dw_ref, db_ref, lg_ref, lb_ref, z_ref, sc_ref, ext_ref, acc_ref, sh_ref,
                     *, tl, dc, first):
    v = u_ref[:, 0:dc] * _sigmoid(u_ref[:, dc:2 * dc])
    vh = uh_ref[:, 0:dc] * _sigmoid(uh_ref[:, dc:2 * dc])
    ext_ref[0:HALO_C, :] = jnp.where(first, 0.0, vh)
    ext_ref[HALO_C:, :] = v
    off = HALO_C - (CONV_C_K - 1)
    rb, a_max = CONV_ROW_BLOCK, (off + CONV_C_K - 1) // 8
    sh_rows = sh_ref.shape[1]
    for b in range(1, 8):
        sh_ref[b - 1] = ext_ref[b:b + sh_rows, :]
    for r0 in range(0, tl, rb):
        for c0 in range(0, dc, LANE):
            lanes = slice(c0, c0 + LANE)
            acc = jnp.broadcast_to(db_ref[:, lanes], (rb, LANE))
            for b in range(8):
                for a in range(a_max + 1):
                    k = 8 * a + b - off
                    if 0 <= k < CONV_C_K:
                        rows = slice(r0 + 8 * a, r0 + 8 * a + rb)
                        win = ext_ref[rows, lanes] if b == 0 else sh_ref[b - 1, rows, lanes]
                        acc = acc + dw_ref[k:k + 1, lanes] * win
            acc_ref[r0:r0 + rb, lanes] = acc
    _ln_swish(acc_ref, z_ref, lg_ref, lb_ref)
    sc_ref[...] = ext_ref[HALO_C + tl - (CONV_C_K - 1):HALO_C + tl, :]


def _odd_prompt(u, conv_sample, dw_w, dw_b, ln_g, ln_b, *, n_batch, seq):
    dc = u.shape[1] // 2
    tl = ROW_TILE
    tiles = seq // tl
    n_tiles = n_batch * tiles
    halo_per_tile = tl // HALO_C
    assert conv_sample.shape == (tl, dc)
    vec = pl.BlockSpec((1, dc), lambda s: (0, 0))

    def tile(s):
        return jnp.minimum(s, n_tiles - 1)

    return pl.pallas_call(
        functools.partial(_odd_prompt_body, tl=tl, dc=dc, tiles=tiles, n_tiles=n_tiles),
        grid=(n_tiles + 1,),
        in_specs=[
            pl.BlockSpec((tl, 2 * dc), lambda s: (tile(s), 0)),
            pl.BlockSpec((HALO_C, 2 * dc), lambda s: (jnp.maximum(tile(s) * halo_per_tile - 1, 0), 0)),
            pl.BlockSpec((tl, dc), lambda s: (0, 0)),
            pl.BlockSpec((CONV_C_K, dc), lambda s: (0, 0)),
            vec, vec, vec,
        ],
        out_specs=[
            pl.BlockSpec((tl, dc), lambda s: (s, 0)),
            pl.BlockSpec((None, CONV_C_K - 1, dc), lambda s: (tile(s) // tiles, 0, 0)),
        ],
        out_shape=[
            jax.ShapeDtypeStruct(((n_tiles + 1) * tl, dc), bf16),
            jax.ShapeDtypeStruct((n_batch, CONV_C_K - 1, dc), f32),
        ],
        scratch_shapes=[pltpu.VMEM((HALO_C + tl, dc), f32), pltpu.VMEM((tl, dc), f32),
                        pltpu.VMEM((7, tl + HALO_C - 8, dc), f32)],
        compiler_params=_params(("arbitrary",)),
        name="odd_prompt",
    )(u, u, conv_sample, dw_w, dw_b.reshape(1, dc), ln_g.reshape(1, dc), ln_b.reshape(1, dc))


def _odd_sample_conv_body(val_ref, gate_ref, sc_ref, dw_ref, db_ref, v_ref, a_ref):
    v = val_ref[...] * _sigmoid(gate_ref[...])
    v_ref[...] = v
    acc = dw_ref[CONV_C_K - 1:CONV_C_K, :] * v + db_ref[...]
    for k in range(CONV_C_K - 1):
        acc = acc + dw_ref[k:k + 1, :] * sc_ref[k]
    a_ref[...] = acc


def _odd_sample(u, state_c_t, dw_w, dw_b, *, row_block):
    dc = u.shape[1] // 2
    nb = state_c_t.shape[1]
    cc = MM_COL_TILE
    nchunk = dc // cc
    return pl.pallas_call(
        _odd_sample_conv_body,
        grid=(nchunk,),
        in_specs=[
            pl.BlockSpec((nb, cc), lambda j: (row_block, j)),
            pl.BlockSpec((nb, cc), lambda j: (row_block, nchunk + j)),
            pl.BlockSpec((CONV_C_K - 1, nb, cc), lambda j: (0, 0, j)),
            pl.BlockSpec((CONV_C_K, cc), lambda j: (0, j)),
            pl.BlockSpec((1, cc), lambda j: (0, j)),
        ],
        out_specs=[pl.BlockSpec((nb, cc), lambda j: (0, j))] * 2,
        out_shape=[jax.ShapeDtypeStruct((nb, dc), f32)] * 2,
        compiler_params=_params(("arbitrary",)),
        name="odd_sample_conv",
    )(u, u, state_c_t, dw_w, dw_b.reshape(1, dc))


def _gather_body(src_ref, h_hbm, o_ref, buf_ref, sem):
    i, n = pl.program_id(0), pl.num_programs(0)

    def row_copy(blk, r, tok):
        slot = lax.rem(blk, 2)
        return pltpu.make_async_copy(h_hbm.at[pl.ds(tok, 1), :], buf_ref.at[slot, pl.ds(r, 1), :], sem.at[slot])

    def for_valid_rows(blk, fn):
        def body(q, carry):
            for s in range(GATHER_ROWS // GATHER_STRIDE):
                r = s * GATHER_STRIDE + q
                tok = src_ref[blk * GATHER_ROWS + r]

                @pl.when(tok >= 0)
                def _():
                    fn(row_copy(blk, r, tok))
            return carry
        lax.fori_loop(0, GATHER_STRIDE, body, 0)

    @pl.when(i == 0)
    def _():
        buf_ref[...] = jnp.zeros(buf_ref.shape, buf_ref.dtype)
        for_valid_rows(0, lambda cp: cp.start())

    @pl.when(i + 1 < n)
    def _():
        for_valid_rows(i + 1, lambda cp: cp.start())

    for_valid_rows(i, lambda cp: cp.wait())
    o_ref[...] = buf_ref[lax.rem(i, 2)].astype(o_ref.dtype)


def _moe_gather(h, src_tok):
    p_total = src_tok.shape[0]
    d = h.shape[1]
    return pl.pallas_call(
        _gather_body,
        grid_spec=pltpu.PrefetchScalarGridSpec(
            num_scalar_prefetch=1,
            grid=(p_total // GATHER_ROWS,),
            in_specs=[pl.BlockSpec(memory_space=pl.ANY)],
            out_specs=pl.BlockSpec((GATHER_ROWS, d), lambda i, src: (i, 0)),
            scratch_shapes=[pltpu.VMEM((2, GATHER_ROWS, d), f32), pltpu.SemaphoreType.DMA((2,))],
        ),
        out_shape=jax.ShapeDtypeStruct((p_total, d), bf16),
        compiler_params=_params(("arbitrary",)),
        name="moe_gather",
    )(src_tok, h)


def _pair_act(h, at_odd):
    n = h.shape[1]
    if at_odd:
        glu_src, lin_src = pltpu.roll(h, 1, axis=1), h
    else:
        glu_src, lin_src = h, pltpu.roll(h, n - 1, axis=1)
    glu = jnp.minimum(glu_src, SWIGLU_LIMIT)
    lin = jnp.clip(lin_src, -SWIGLU_LIMIT, SWIGLU_LIMIT)
    return glu * _sigmoid(SWIGLU_ALPHA * glu) * (lin + 1.0)


def _expert_body(se_ref, sblk_ref, snc_ref, tail_ref, x_ref, w1_ref, b1_ref, wd_ref, bd_ref, ys_hbm,
                 w1b_ref, wdb_ref, act_ref, perm_ref, obuf_ref, osem, *, j1, k2):
    del se_ref
    i, j = pl.program_id(0), pl.program_id(1)
    n_chunks = snc_ref[i]
    half = MOE_C1 // 2
    ch = MOE_CHUNK
    row0 = sblk_ref[i] * ch

    def for_row_blocks(fn):
        done = 0
        for size in MOE_BLOCKS:
            count = (n_chunks - done) // size
            rows = size * ch

            def body(b, carry, done=done, rows=rows):
                fn(pl.multiple_of(done * ch + b * rows, ch), rows)
                return carry

            lax.fori_loop(0, count, body, 0)
            done = done + count * size

    def out_copy(src_slot, src_row, rows, dst_row, col_tile):
        return pltpu.make_async_copy(
            obuf_ref.at[src_slot, pl.ds(src_row, rows), :],
            ys_hbm.at[pl.ds(pl.multiple_of(dst_row, ch), rows), pl.ds(pl.multiple_of(col_tile * MOE_C2, MOE_C2), MOE_C2)],
            osem.at[src_slot])

    def wait_tile(slot, col_tile):
        for_row_blocks(lambda r0, rows: out_copy(slot, r0, rows, row0 + r0, col_tile).wait())

    @pl.when(j < j1)
    def _gate_up():
        w1b_ref[...] = w1_ref[...].astype(bf16)

        def rows_fn(r0, rows):
            even = (lax.broadcasted_iota(i32, (rows, half), 1) & 1) == 0
            hh = jnp.dot(x_ref[pl.ds(r0, rows), :], w1b_ref[...], preferred_element_type=f32) + b1_ref[...]
            act = jnp.where(even, _pair_act(hh[:, :half], at_odd=False), _pair_act(hh[:, half:], at_odd=True))
            act_ref[j, pl.ds(r0, rows), :] = act.astype(bf16)

        for_row_blocks(rows_fn)

    @pl.when(j >= j1)
    def _down():
        k = j - j1
        slot = lax.rem(k, 2)

        @pl.when(k >= 2)
        def _():
            wait_tile(slot, k - 2)

        for jj in range(j1):
            for c in range(MOE_C2 // LANE):
                rows = slice(jj * half, (jj + 1) * half)
                cols = slice(c * LANE, (c + 1) * LANE)
                perm_ref[c, pl.ds(0, half // 2, stride=2), :] = wd_ref[jj * half:jj * half + half // 2, cols]
                perm_ref[c, pl.ds(1, half // 2, stride=2), :] = wd_ref[jj * half + half // 2:(jj + 1) * half, cols]
                wdb_ref[rows, cols] = perm_ref[c].astype(bf16)

        def rows_fn(r0, rows):
            a = jnp.concatenate([act_ref[jj, pl.ds(r0, rows), :] for jj in range(j1)], axis=1)
            obuf_ref[slot, pl.ds(r0, rows), :] = jnp.dot(a, wdb_ref[...], preferred_element_type=f32) + bd_ref[...]
            out_copy(slot, r0, rows, row0 + r0, k).start()

        for_row_blocks(rows_fn)

        @pl.when(k == k2 - 1)
        def _drain():
            wait_tile(1 - slot, k - 1)
            wait_tile(slot, k)

            @pl.when(i == pl.num_programs(0) - 1)
            def _tail():
                obuf_ref[0, 0:ch, :] = jnp.zeros((ch, MOE_C2), f32)
                tail_row0, n_tail = tail_ref[0] * ch, tail_ref[1]

                def start(t, carry):
                    for kk in range(k2):
                        out_copy(0, 0, ch, tail_row0 + t * ch, kk).start()
                    return carry

                def wait(t, carry):
                    for kk in range(k2):
                        out_copy(0, 0, ch, tail_row0 + t * ch, kk).wait()
                    return carry

                lax.fori_loop(0, n_tail, start, 0)
                lax.fori_loop(0, n_tail, wait, 0)


def _moe_experts(xs, w_gate_up, b_gate_up, w_down, b_down, layer, ys_rows, n_super, sup_e, sup_blk, sup_nc, tail):
    d = xs.shape[1]
    de = w_down.shape[2]
    j1 = (2 * de) // MOE_C1
    k2 = d // MOE_C2
    assert k2 >= 2
    s = MOE_SUPER
    el = pl.Element

    def w1_idx(i, j, se, sb, nc, tl):
        return (layer, se[i], 0, jnp.minimum(j, j1 - 1))

    def wd_idx(i, j, se, sb, nc, tl):
        return (layer, se[i], 0, jnp.maximum(j - j1, 0))

    return pl.pallas_call(
        functools.partial(_expert_body, j1=j1, k2=k2),
        grid_spec=pltpu.PrefetchScalarGridSpec(
            num_scalar_prefetch=4,
            grid=(n_super, j1 + k2),
            in_specs=[
                pl.BlockSpec((el(s), el(d)), lambda i, j, se, sb, nc, tl: (sb[i] * MOE_CHUNK, 0),
                             pipeline_mode=pl.Buffered(1)),
                pl.BlockSpec((None, None, d, MOE_C1), w1_idx),
                pl.BlockSpec((None, None, 1, MOE_C1), w1_idx),
                pl.BlockSpec((None, None, de, MOE_C2), wd_idx),
                pl.BlockSpec((None, None, 1, MOE_C2), wd_idx),
            ],
            out_specs=pl.BlockSpec(memory_space=pl.ANY),
            scratch_shapes=[
                pltpu.VMEM((d, MOE_C1), bf16),
                pltpu.VMEM((de, MOE_C2), bf16),
                pltpu.VMEM((j1, s, MOE_C1 // 2), bf16),
                pltpu.VMEM((MOE_C2 // LANE, MOE_C1 // 2, LANE), f32),
                pltpu.VMEM((2, s, MOE_C2), f32),
                pltpu.SemaphoreType.DMA((2,)),
            ],
        ),
        out_shape=jax.ShapeDtypeStruct((ys_rows, d), f32),
        compiler_params=_params(("arbitrary", "arbitrary")),
        name="moe_experts",
    )(sup_e, sup_blk, sup_nc, tail, xs, w_gate_up, b_gate_up.reshape(*b_gate_up.shape[:2], 1, -1),
      w_down, b_down.reshape(*b_down.shape[:2], 1, -1))


def _combine_body(dest_ref, ys_hbm, gate_ref, o_ref, buf_ref, sem):
    tm = o_ref.shape[0]
    i, n = pl.program_id(0), pl.num_programs(0)

    def row_copy(tile, r, k):
        slot = lax.rem(tile, 2)
        p = dest_ref[(tile * tm + r) * TOP_K + k]
        return pltpu.make_async_copy(ys_hbm.at[pl.ds(p, 1), :], buf_ref.at[slot, k, pl.ds(r, 1), :], sem.at[slot])

    def start_tile(tile):
        def body(r, carry):
            for k in range(TOP_K):
                row_copy(tile, r, k).start()
            return carry
        lax.fori_loop(0, tm, body, 0, unroll=4)

    @pl.when(i == 0)
    def _():
        start_tile(0)

    @pl.when(i + 1 < n)
    def _():
        start_tile(i + 1)

    def wait(r, carry):
        for k in range(TOP_K):
            row_copy(i, r, k).wait()
        return carry

    lax.fori_loop(0, tm, wait, 0, unroll=4)
    slot = lax.rem(i, 2)
    g = gate_ref[...]
    acc = buf_ref[slot, 0] * g[:, 0:1]
    for k in range(1, TOP_K):
        acc = acc + buf_ref[slot, k] * g[:, k:k + 1]
    o_ref[...] = acc


def _moe_combine(ys, dest, gates, n_rows):
    d = ys.shape[1]
    tm = ROW_TILE
    return pl.pallas_call(
        _combine_body,
        grid_spec=pltpu.PrefetchScalarGridSpec(
            num_scalar_prefetch=1,
            grid=(n_rows // tm,),
            in_specs=[pl.BlockSpec(memory_space=pl.ANY), pl.BlockSpec((tm, LANE), lambda i, dest: (i, 0))],
            out_specs=pl.BlockSpec((tm, d), lambda i, dest: (i, 0)),
            scratch_shapes=[pltpu.VMEM((2, TOP_K, tm, d), f32), pltpu.SemaphoreType.DMA((2,))],
        ),
        out_shape=jax.ShapeDtypeStruct((n_rows, d), f32),
        compiler_params=_params(("arbitrary",)),
        name="moe_combine",
    )(dest, ys, gates)


def _moe_plan(top_idx, n_exp):
    t = top_idx.shape[0]
    n_assign = t * TOP_K
    flat_e = top_idx.reshape(-1)
    order = jnp.argsort(flat_e).astype(i32)
    rank = jnp.argsort(order).astype(i32)
    counts = jnp.sum((flat_e[:, None] == jnp.arange(n_exp, dtype=i32)[None, :]).astype(i32), axis=0)
    padded = (counts + MOE_CHUNK - 1) // MOE_CHUNK * MOE_CHUNK
    start = jnp.cumsum(counts) - counts
    pstart = jnp.cumsum(padded) - padded
    dest = (pstart[flat_e] + rank - start[flat_e]).astype(i32)
    max_rows = -(-(n_assign + n_exp * (MOE_CHUNK - 1)) // MOE_CHUNK) * MOE_CHUNK
    p_total = -(-(max_rows + MOE_SUPER) // GATHER_ROWS) * GATHER_ROWS
    p = jnp.arange(p_total, dtype=i32)
    e_of_p = jnp.minimum(jnp.sum((p[:, None] >= (pstart + padded)[None, :]).astype(i32), axis=1), n_exp - 1)
    in_group = p - pstart[e_of_p]
    src_assign = order[jnp.clip(start[e_of_p] + in_group, 0, n_assign - 1)]
    src_tok = jnp.where(in_group < counts[e_of_p], src_assign // TOP_K, -1).astype(i32)
    n_sup_e = (padded + MOE_SUPER - 1) // MOE_SUPER
    cum = jnp.cumsum(n_sup_e)
    max_super = n_exp + max_rows // MOE_SUPER
    sid = jnp.arange(max_super, dtype=i32)
    sup_e = jnp.minimum(jnp.searchsorted(cum, sid, side="right"), n_exp - 1).astype(i32)
    within = sid - (cum[sup_e] - n_sup_e[sup_e])
    row0 = pstart[sup_e] + within * MOE_SUPER
    rows = jnp.clip(padded[sup_e] - within * MOE_SUPER, 0, MOE_SUPER)
    rows = jnp.where(sid < cum[-1], rows, 0)
    used = jnp.sum(padded)
    tail = jnp.stack([used // MOE_CHUNK, (max_rows - used) // MOE_CHUNK]).astype(i32)
    return dict(dest=dest, src_tok=src_tok, n_super=cum[-1].astype(i32), sup_e=sup_e, ys_rows=max_rows,
                sup_blk=(row0 // MOE_CHUNK).astype(i32), sup_nc=(rows // MOE_CHUNK).astype(i32), tail=tail)


def _moe(h, top_idx, gates, w_gate_up, b_gate_up, w_down, b_down, layer):
    plan = _moe_plan(top_idx[:, :TOP_K], w_down.shape[1])
    xs = _moe_gather(h, plan["src_tok"])
    ys = _moe_experts(xs, w_gate_up, b_gate_up, w_down, b_down, layer, plan["ys_rows"],
                      plan["n_super"], plan["sup_e"], plan["sup_blk"], plan["sup_nc"], plan["tail"])
    return _moe_combine(ys, plan["dest"], gates, h.shape[0])


def kernel(x_prompt, x_sample, c_prompt, c_sample, state_conv_a, state_pool_b, state_conv_c, w_ada, b_ada, norm_gains, ab_w_in, a_conv_w, b_pool_w, b_pool_scale, ab_w_out, c_pw1_w, c_pw1_b, c_dw_w, c_dw_b, c_ln_g, c_ln_b, c_pw2_w, c_pw2_b, moe_w_router, moe_b_router, moe_w_gate_up, moe_b_gate_up, moe_w_down, moe_b_down):
    n_batch, seq, d = x_prompt.shape
    n_dec, dec_seq, _ = x_sample.shape
    depth = w_ada.shape[0]
    assert dec_seq == 1 and n_dec == ROW_TILE and seq % ROW_TILE == 0
    n_prompt_rows = n_batch * seq
    r = n_prompt_rows + n_dec
    assert r % MM_ROW_TILE == 0
    sample_block = n_prompt_rows // ROW_TILE

    x = (x_prompt.reshape(n_prompt_rows, d), x_sample.reshape(n_dec, d))
    pad = (-(n_dec + n_batch)) % 8
    c_all = jnp.concatenate([c_sample, c_prompt, jnp.zeros((pad, d), f32)], axis=0)
    mod_all = _ada(c_all, w_ada, b_ada)
    rowwise = functools.partial(_rowwise, n_prompt_rows=n_prompt_rows, rows_per_seq=seq)

    new_a_p, new_b_p, new_c_p, new_a_s, new_b_s, new_c_s = [], [], [], [], [], []
    h = None
    for l in range(depth):
        mod_s = mod_all[l, :n_dec]
        mod_p = mod_all[l, n_dec:n_dec + n_batch].reshape(n_batch, 1, N_MOD * d)
        ng = norm_gains[l]
        if l == 0:
            h = rowwise(x, mod_s, mod_p, norm=(ng[0], 1, 0))[0]
        if l % 2 == 0:
            e = l // 2
            proj = _mm(h, ab_w_in[e])
            y_s, g_s = _even_sample(proj, state_conv_a[e].transpose(1, 0, 2), state_pool_b[e].transpose(1, 0, 2),
                                    a_conv_w[e], b_pool_w[e], b_pool_scale[e], row_block=sample_block)
            ycat, sa, sb = _even_prompt(proj, y_s, a_conv_w[e], b_pool_w[e], b_pool_scale[e],
                                        n_batch=n_batch, seq=seq)
            y = _mm(ycat, ab_w_out[e])
            new_a_p.append(sa)
            new_b_p.append(sb)
            new_a_s.append(jnp.concatenate([state_conv_a[e][:, 1:], g_s[:, None]], axis=1))
            b_in_s = proj[n_prompt_rows:, 3 * (d // 2):]
            new_b_s.append(jnp.concatenate([state_pool_b[e][:, 1:], b_in_s[:, None]], axis=1))
        else:
            o = l // 2
            u = _mm(h, c_pw1_w[o], c_pw1_b[o])
            v_s, conv_s = _odd_sample(u, state_conv_c[o].transpose(1, 0, 2), c_dw_w[o], c_dw_b[o],
                                      row_block=sample_block)
            zcat, s_c = _odd_prompt(u, conv_s, c_dw_w[o], c_dw_b[o], c_ln_g[o], c_ln_b[o],
                                    n_batch=n_batch, seq=seq)
            y = _mm(zcat, c_pw2_w[o], c_pw2_b[o])
            new_c_p.append(s_c)
            new_c_s.append(jnp.concatenate([state_conv_c[o][:, 1:], v_s[:, None]], axis=1))
        x, h2, top_idx, gates = rowwise(x, mod_s, mod_p, resid=(y, ng[1], 2), norm=(ng[2], 4, 3),
                                        router=(moe_w_router[l], moe_b_router[l]), h_dtype=f32)
        y = _moe(h2, top_idx, gates, moe_w_gate_up, moe_b_gate_up, moe_w_down, moe_b_down, l)
        if l + 1 < depth:
            mod_s_n = mod_all[l + 1, :n_dec]
            mod_p_n = mod_all[l + 1, n_dec:n_dec + n_batch].reshape(n_batch, 1, N_MOD * d)
            x, h = rowwise(x, mod_s, mod_p, resid=(y, ng[3], 5), norm=(norm_gains[l + 1][0], 1, 0),
                           norm_mods=(mod_s_n, mod_p_n))
        else:
            x = rowwise(x, mod_s, mod_p, resid=(y, ng[3], 5), split_out=True)

    y_prompt = x[0].reshape(n_batch, seq, d)
    y_sample = x[1].reshape(n_dec, 1, d)
    return (y_prompt, y_sample, jnp.stack(new_a_p), jnp.stack(new_b_p), jnp.stack(new_c_p),
            jnp.stack(new_a_s), jnp.stack(new_b_s), jnp.stack(new_c_s))
```

```python
import functools

import jax
import jax.numpy as jnp
from jax import lax
from jax.experimental import pallas as pl
from jax.experimental.pallas import tpu as pltpu

f32 = jnp.float32
bf16 = jnp.bfloat16
i32 = jnp.int32

LANE = 128
V7X_VMEM_LIMIT_BYTES = 56 * 1024 * 1024

POOL_WINDOWS = (2, 4, 8, 16)
POOL_BUF = max(POOL_WINDOWS) - 1
CONV_A_K = 3
CONV_C_K = 31
TOP_K = 4
SWIGLU_LIMIT = 7.0
SWIGLU_ALPHA = 1.702
N_MOD = 6
RMS_EPS = 1e-6
LN_EPS = 1e-5
PAST_LEN = 16384

ROW_TILE = 128
MM_ROW_TILE = 1664
MM_COL_TILE = 512
HALO_A = 16
HALO_C = 32
CONV_ROW_BLOCK = 64
LN_ROW_BLOCK = 16
MOE_CHUNK = 128
MOE_BLOCKS = (4, 1)
MOE_SUPER = 12 * MOE_CHUNK
MOE_C1 = 512
MOE_C2 = 512
GATHER_ROWS = 256
GATHER_STRIDE = 32


def _params(sem, vmem=V7X_VMEM_LIMIT_BYTES):
    return pltpu.CompilerParams(dimension_semantics=sem, vmem_limit_bytes=vmem)


def _sigmoid(x):
    return 1.0 / (1.0 + jnp.exp(-x))


def _ada_body(c_ref, w_ref, b_ref, o_ref):
    c = c_ref[...]
    ca = (c * _sigmoid(c)).astype(bf16)
    o_ref[...] = jnp.dot(ca, w_ref[...].astype(bf16), preferred_element_type=f32) + b_ref[...]


def _ada(c_all, w_ada, b_ada):
    depth, d, n = w_ada.shape
    rows = c_all.shape[0]
    tn = MM_COL_TILE
    return pl.pallas_call(
        _ada_body,
        grid=(depth, n // tn),
        in_specs=[
            pl.BlockSpec((rows, d), lambda l, j: (0, 0)),
            pl.BlockSpec((None, d, tn), lambda l, j: (l, 0, j)),
            pl.BlockSpec((None, 1, tn), lambda l, j: (l, 0, j)),
        ],
        out_specs=pl.BlockSpec((None, rows, tn), lambda l, j: (l, 0, j)),
        out_shape=jax.ShapeDtypeStruct((depth, rows, n), f32),
        compiler_params=_params(("arbitrary", "arbitrary")),
        name="ada",
    )(c_all, w_ada, b_ada.reshape(depth, 1, n))


def _rms(x, g):
    return x * lax.rsqrt(jnp.mean(x * x, axis=-1, keepdims=True) + RMS_EPS) * g


def _rowwise_body(*refs, n_prompt_tiles, has_resid, has_norm, has_router, split_in, split_out):
    refs = list(refs)
    is_sample = pl.program_id(0) == n_prompt_tiles

    def mod(ref_s, ref_p):
        return jnp.where(is_sample, ref_s[...], ref_p[0])

    if split_in:
        xp_ref, xs_ref = refs.pop(0), refs.pop(0)
        x = jnp.where(is_sample, xs_ref[...], xp_ref[...])
    else:
        x = refs.pop(0)[...]
    if has_resid:
        y_ref, gy_ref, gate_s, gate_p = refs[:4]
        refs = refs[4:]
        x = x + mod(gate_s, gate_p) * _rms(y_ref[...], gy_ref[...])
    if has_norm:
        gh_ref, scale_s, scale_p, shift_s, shift_p = refs[:5]
        refs = refs[5:]
        h = _rms(x, gh_ref[...]) * (1.0 + mod(scale_s, scale_p)) + mod(shift_s, shift_p)
    if has_router:
        wr_ref, br_ref = refs[:2]
        refs = refs[2:]
    outs = refs
    if has_resid and split_out:
        xo_p, xo_s = outs.pop(0), outs.pop(0)

        @pl.when(is_sample)
        def _():
            xo_s[...] = x

        @pl.when(jnp.logical_not(is_sample))
        def _():
            xo_p[...] = x
    elif has_resid:
        outs.pop(0)[...] = x
    if has_norm:
        h_ref = outs.pop(0)
        h_ref[...] = h.astype(h_ref.dtype)
    if has_router:
        idx_ref, gate_ref = outs
        wr = wr_ref[...]
        h_hi, wr_hi = h.astype(bf16), wr.astype(bf16)
        h_lo = (h - h_hi.astype(f32)).astype(bf16)
        wr_lo = (wr - wr_hi.astype(f32)).astype(bf16)
        logits = (jnp.dot(h_hi, wr_hi, preferred_element_type=f32) + jnp.dot(h_hi, wr_lo, preferred_element_type=f32)
                  + jnp.dot(h_lo, wr_hi, preferred_element_type=f32) + br_ref[...])
        n_exp = logits.shape[-1]
        lane = lax.broadcasted_iota(i32, logits.shape, 1)
        vals, idxs = [], []
        for _ in range(TOP_K):
            m = jnp.max(logits, axis=-1, keepdims=True)
            idx = jnp.min(jnp.where(logits == m, lane, n_exp), axis=-1, keepdims=True)
            vals.append(m)
            idxs.append(idx)
            logits = jnp.where(lane == idx, -jnp.inf, logits)
        es = [jnp.exp(v - vals[0]) for v in vals]
        denom = es[0] + es[1] + es[2] + es[3]
        out_lane = lax.broadcasted_iota(i32, idx_ref.shape, 1)
        idx_out = jnp.zeros(idx_ref.shape, i32)
        gate_out = jnp.zeros(gate_ref.shape, f32)
        for k in range(TOP_K):
            idx_out = jnp.where(out_lane == k, idxs[k], idx_out)
            gate_out = jnp.where(out_lane == k, es[k] / denom, gate_out)
        idx_ref[...] = idx_out
        gate_ref[...] = gate_out


def _rowwise(x, mod_s, mod_p, *, n_prompt_rows, rows_per_seq, resid=None, norm=None, router=None, h_dtype=bf16,
             norm_mods=None, split_out=False):
    tm = ROW_TILE
    n_prompt_tiles = n_prompt_rows // tm
    tiles_per_seq = rows_per_seq // tm
    n_batch = mod_p.shape[0]
    split_in = isinstance(x, tuple)
    d = x[0].shape[1] if split_in else x.shape[1]
    r = n_prompt_rows + tm
    row_spec = pl.BlockSpec((tm, d), lambda i: (i, 0))
    prompt_spec = pl.BlockSpec((tm, d), lambda i: (jnp.minimum(i, n_prompt_tiles - 1), 0))
    sample_spec = pl.BlockSpec((tm, d), lambda i: (0, 0))
    vec_spec = pl.BlockSpec((1, d), lambda i: (0, 0))

    def mod_specs(col):
        return [
            pl.BlockSpec((mod_s.shape[0], d), lambda i: (0, col)),
            pl.BlockSpec((1, 1, d), lambda i: (jnp.minimum(i // tiles_per_seq, n_batch - 1), 0, col)),
        ]

    out_shape, out_specs = [], []
    if split_in:
        args, in_specs = list(x), [prompt_spec, sample_spec]
    else:
        args, in_specs = [x], [row_spec]
    if resid is not None:
        y, gain_y, gate_col = resid
        args += [y, gain_y.reshape(1, d), mod_s, mod_p]
        in_specs += [row_spec, vec_spec] + mod_specs(gate_col)
        if split_out:
            out_shape += [jax.ShapeDtypeStruct((n_prompt_rows, d), f32), jax.ShapeDtypeStruct((tm, d), f32)]
            out_specs += [prompt_spec, sample_spec]
        else:
            out_shape.append(jax.ShapeDtypeStruct((r, d), f32))
            out_specs.append(row_spec)
    if norm is not None:
        gain_h, scale_col, shift_col = norm
        nm_s, nm_p = (mod_s, mod_p) if norm_mods is None else norm_mods
        args += [gain_h.reshape(1, d), nm_s, nm_p, nm_s, nm_p]
        in_specs += [vec_spec] + mod_specs(scale_col) + mod_specs(shift_col)
        out_shape.append(jax.ShapeDtypeStruct((r, d), h_dtype))
        out_specs.append(row_spec)
    if router is not None:
        w_router, b_router = router
        n_exp = w_router.shape[1]
        args += [w_router, b_router.reshape(1, n_exp)]
        in_specs += [pl.BlockSpec((d, n_exp), lambda i: (0, 0)), pl.BlockSpec((1, n_exp), lambda i: (0, 0))]
        out_shape += [jax.ShapeDtypeStruct((r, LANE), i32), jax.ShapeDtypeStruct((r, LANE), f32)]
        out_specs += [pl.BlockSpec((tm, LANE), lambda i: (i, 0))] * 2
    body = functools.partial(_rowwise_body, n_prompt_tiles=n_prompt_tiles, has_resid=resid is not None,
                             has_norm=norm is not None, has_router=router is not None,
                             split_in=split_in, split_out=split_out)
    return pl.pallas_call(
        body, grid=(r // tm,), in_specs=in_specs, out_specs=out_specs, out_shape=out_shape,
        compiler_params=_params(("arbitrary",)), name="rowwise",
    )(*args)


def _mm_body(*refs, has_bias):
    if has_bias:
        x_ref, w_ref, b_ref, o_ref, wb_ref = refs
    else:
        x_ref, w_ref, o_ref, wb_ref = refs

    @pl.when(pl.program_id(1) == 0)
    def _():
        wb_ref[...] = w_ref[...].astype(bf16)

    acc = jnp.dot(x_ref[...], wb_ref[...], preferred_element_type=f32)
    if has_bias:
        acc = acc + b_ref[...]
    o_ref[...] = acc


def _mm(x, w, bias=None):
    m, k = x.shape
    n = w.shape[1]
    tm, tn = MM_ROW_TILE, MM_COL_TILE
    args = [x, w]
    in_specs = [pl.BlockSpec((tm, k), lambda j, i: (i, 0)), pl.BlockSpec((k, tn), lambda j, i: (0, j))]
    if bias is not None:
        args.append(bias.reshape(1, n))
        in_specs.append(pl.BlockSpec((1, tn), lambda j, i: (0, j)))
    return pl.pallas_call(
        functools.partial(_mm_body, has_bias=bias is not None),
        grid=(n // tn, m // tm),
        in_specs=in_specs,
        out_specs=pl.BlockSpec((tm, tn), lambda j, i: (i, j)),
        out_shape=jax.ShapeDtypeStruct((m, n), f32),
        scratch_shapes=[pltpu.VMEM((k, tn), bf16)],
        compiler_params=_params(("arbitrary", "arbitrary")),
        name="mm",
    )(*args)


def _pool_mix(z, pw_ref, ps_ref, gi, c0, cw):
    yb = jnp.dot(z.astype(bf16), pw_ref[gi].astype(bf16), preferred_element_type=f32)
    return yb * ps_ref[:, c0:c0 + cw]


def _even_prompt_body(p_ref, ph_ref, ysm_ref, cw_ref, pw_ref, ps_ref, y_ref, sa_ref, sb_ref, eg_ref, eb_ref,
                      *, tl, da, tiles, n_tiles):
    step = pl.program_id(0)

    @pl.when(step == n_tiles)
    def _sample_rows():
        y_ref[...] = ysm_ref[...]

    @pl.when(step < n_tiles)
    def _prompt_rows():
        _even_prompt_tile(p_ref, ph_ref, cw_ref, pw_ref, ps_ref, y_ref, sa_ref, sb_ref, eg_ref, eb_ref,
                          tl=tl, da=da, tile_in_seq=step % tiles)


def _even_prompt_tile(p_ref, ph_ref, cw_ref, pw_ref, ps_ref, y_ref, sa_ref, sb_ref, eg_ref, eb_ref,
                      *, tl, da, tile_in_seq):
    first = tile_in_seq == 0
    a_pre, a_post = p_ref[:, 0:da], p_ref[:, da:2 * da]
    a_val, b_in = p_ref[:, 2 * da:3 * da], p_ref[:, 3 * da:4 * da]
    g = a_pre * a_val
    eg_ref[0:HALO_A, :] = jnp.where(first, 0.0, ph_ref[:, 0:da] * ph_ref[:, 2 * da:3 * da])
    eg_ref[HALO_A:, :] = g
    eb_ref[0:HALO_A, :] = jnp.where(first, 0.0, ph_ref[:, 3 * da:4 * da])
    eb_ref[HALO_A:, :] = b_in
    conv = (cw_ref[0:1, :] * eg_ref[HALO_A - 2:HALO_A - 2 + tl, :]
            + cw_ref[1:2, :] * eg_ref[HALO_A - 1:HALO_A - 1 + tl, :]
            + cw_ref[2:3, :] * g)
    y_ref[:, 0:da] = (a_post * conv).astype(y_ref.dtype)
    sa_ref[...] = eg_ref[HALO_A + tl - (CONV_A_K - 1):HALO_A + tl, :]
    sb_ref[...] = eb_ref[HALO_A + tl - POOL_BUF:HALO_A + tl, :]
    pos = tile_in_seq * tl + lax.broadcasted_iota(i32, (tl, 1), 0)
    cw = da // len(POOL_WINDOWS)
    for gi, w in enumerate(POOL_WINDOWS):
        c0 = gi * cw
        cur = b_in[:, c0:c0 + cw]
        s = cur
        for back in range(1, w):
            s = s + eb_ref[HALO_A - back:HALO_A - back + tl, c0:c0 + cw]
        count = jnp.minimum(pos + 1, w).astype(f32)
        z = s / count - cur
        y_ref[:, da + c0:da + c0 + cw] = _pool_mix(z, pw_ref, ps_ref, gi, c0, cw).astype(y_ref.dtype)


def _even_prompt(proj, y_sample, conv_w, pool_w, pool_scale, *, n_batch, seq):
    d_in = proj.shape[1]
    da = d_in // 4
    tl = ROW_TILE
    tiles = seq // tl
    n_tiles = n_batch * tiles
    halo_per_tile = tl // HALO_A
    ng, cg = pool_w.shape[0], pool_w.shape[1]
    assert y_sample.shape == (tl, 2 * da)

    def tile(s):
        return jnp.minimum(s, n_tiles - 1)

    def batch(s):
        return tile(s) // tiles

    return pl.pallas_call(
        functools.partial(_even_prompt_body, tl=tl, da=da, tiles=tiles, n_tiles=n_tiles),
        grid=(n_tiles + 1,),
        in_specs=[
            pl.BlockSpec((tl, d_in), lambda s: (tile(s), 0)),
            pl.BlockSpec((HALO_A, d_in), lambda s: (jnp.maximum(tile(s) * halo_per_tile - 1, 0), 0)),
            pl.BlockSpec((tl, 2 * da), lambda s: (0, 0)),
            pl.BlockSpec((CONV_A_K, da), lambda s: (0, 0)),
            pl.BlockSpec((ng, cg, cg), lambda s: (0, 0, 0)),
            pl.BlockSpec((1, da), lambda s: (0, 0)),
        ],
        out_specs=[
            pl.BlockSpec((tl, 2 * da), lambda s: (s, 0)),
            pl.BlockSpec((None, CONV_A_K - 1, da), lambda s: (batch(s), 0, 0)),
            pl.BlockSpec((None, POOL_BUF, da), lambda s: (batch(s), 0, 0)),
        ],
        out_shape=[
            jax.ShapeDtypeStruct(((n_tiles + 1) * tl, 2 * da), bf16),
            jax.ShapeDtypeStruct((n_batch, CONV_A_K - 1, da), f32),
            jax.ShapeDtypeStruct((n_batch, POOL_BUF, da), f32),
        ],
        scratch_shapes=[pltpu.VMEM((HALO_A + tl, da), f32), pltpu.VMEM((HALO_A + tl, da), f32)],
        compiler_params=_params(("arbitrary",)),
        name="even_prompt",
    )(proj, proj, y_sample, conv_w, pool_w, pool_scale.reshape(1, da))


def _even_sample_body(apre_ref, apost_ref, aval_ref, bin_ref, sa_ref, sb_ref, cw_ref, pw_ref, ps_ref,
                      y_ref, g_ref, *, ng):
    s_id = pl.program_id(0)

    @pl.when(s_id < ng)
    def _conv():
        g = apre_ref[...] * aval_ref[...]
        g_ref[...] = g
        conv = cw_ref[0:1, :] * sa_ref[0] + cw_ref[1:2, :] * sa_ref[1] + cw_ref[2:3, :] * g
        y_ref[...] = (apost_ref[...] * conv).astype(y_ref.dtype)

    @pl.when(s_id >= ng)
    def _pool():
        cur = bin_ref[...]
        s = cur
        z = jnp.zeros_like(cur)
        back = 1
        for gi, w in enumerate(POOL_WINDOWS):
            while back < w:
                s = s + sb_ref[POOL_BUF - back]
                back += 1
            count = float(min(PAST_LEN + 1, w))
            z = jnp.where(s_id - ng == gi, s / count - cur, z)
        yb = jnp.dot(z.astype(bf16), pw_ref[...].astype(bf16), preferred_element_type=f32) * ps_ref[...]
        y_ref[...] = yb.astype(y_ref.dtype)


def _even_sample(proj, state_a_t, state_b_t, conv_w, pool_w, pool_scale, *, row_block):
    da = proj.shape[1] // 4
    ng, cg = pool_w.shape[0], pool_w.shape[1]
    nb = state_a_t.shape[1]
    assert nb == ROW_TILE and cg * ng == da and ng == len(POOL_WINDOWS)

    def conv_chunk(s):
        return jnp.minimum(s, ng - 1)

    def pool_group(s):
        return jnp.maximum(s - ng, 0)

    def col(k):
        return pl.BlockSpec((nb, cg), lambda s: (row_block, k * ng + conv_chunk(s)))

    return pl.pallas_call(
        functools.partial(_even_sample_body, ng=ng),
        grid=(2 * ng,),
        in_specs=[
            col(0), col(1), col(2),
            pl.BlockSpec((nb, cg), lambda s: (row_block, 3 * ng + pool_group(s))),
            pl.BlockSpec((CONV_A_K - 1, nb, cg), lambda s: (0, 0, conv_chunk(s))),
            pl.BlockSpec((POOL_BUF, nb, cg), lambda s: (0, 0, pool_group(s))),
            pl.BlockSpec((CONV_A_K, cg), lambda s: (0, conv_chunk(s))),
            pl.BlockSpec((None, cg, cg), lambda s: (pool_group(s), 0, 0)),
            pl.BlockSpec((1, cg), lambda s: (0, pool_group(s))),
        ],
        out_specs=[
            pl.BlockSpec((nb, cg), lambda s: (0, s)),
            pl.BlockSpec((nb, cg), lambda s: (0, conv_chunk(s))),
        ],
        out_shape=[
            jax.ShapeDtypeStruct((nb, 2 * da), bf16),
            jax.ShapeDtypeStruct((nb, da), f32),
        ],
        compiler_params=_params(("arbitrary",)),
        name="even_sample",
    )(proj, proj, proj, proj, state_a_t, state_b_t, conv_w, pool_w, pool_scale.reshape(1, da))


def _ln_swish(src_ref, dst_ref, g_ref, b_ref):
    rows = LN_ROW_BLOCK

    def block(i, carry):
        r = pl.ds(pl.multiple_of(i * rows, rows), rows)
        mu = jnp.mean(src_ref[r, :], axis=-1, keepdims=True)
        zc = src_ref[r, :] - mu
        var = jnp.mean(zc * zc, axis=-1, keepdims=True)
        y = (src_ref[r, :] - mu) * lax.rsqrt(var + LN_EPS) * g_ref[...] + b_ref[...]
        dst_ref[r, :] = (y * _sigmoid(y)).astype(dst_ref.dtype)
        return carry

    lax.fori_loop(0, src_ref.shape[0] // rows, block, 0)


def _odd_prompt_body(u_ref, uh_ref, zsm_ref, dw_ref, db_ref, lg_ref, lb_ref, z_ref, sc_ref, ext_ref, acc_ref, sh_ref,
                     *, tl, dc, tiles, n_tiles):
    step = pl.program_id(0)

    @pl.when(step == n_tiles)
    def _sample_rows():
        _ln_swish(zsm_ref, z_ref, lg_ref, lb_ref)

    @pl.when(step < n_tiles)
    def _prompt_rows():
        _odd_prompt_tile(u_ref, uh_ref, dw_ref, db_ref, lg_ref, lb_ref, z_ref, sc_ref, ext_ref, acc_ref, sh_ref,
                         tl=tl, dc=dc, first=step % tiles == 0)


def _odd_prompt_tile(u_ref, uh_ref, dw_ref, db_ref, lg_ref, lb_ref, z_ref, sc_ref, ext_ref, acc_ref, sh_ref,
                     *, tl, dc, first):
    v = u_ref[:, 0:dc] * _sigmoid(u_ref[:, dc:2 * dc])
    vh = uh_ref[:, 0:dc] * _sigmoid(uh_ref[:, dc:2 * dc])
    ext_ref[0:HALO_C, :] = jnp.where(first, 0.0, vh)
    ext_ref[HALO_C:, :] = v
    off = HALO_C - (CONV_C_K - 1)
    rb, a_max = CONV_ROW_BLOCK, (off + CONV_C_K - 1) // 8
    sh_rows = sh_ref.shape[1]
    for b in range(1, 8):
        sh_ref[b - 1] = ext_ref[b:b + sh_rows, :]
    for r0 in range(0, tl, rb):
        for c0 in range(0, dc, LANE):
            lanes = slice(c0, c0 + LANE)
            acc = jnp.broadcast_to(db_ref[:, lanes], (rb, LANE))
            for b in range(8):
                for a in range(a_max + 1):
                    k = 8 * a + b - off
                    if 0 <= k < CONV_C_K:
                        rows = slice(r0 + 8 * a, r0 + 8 * a + rb)
                        win = ext_ref[rows, lanes] if b == 0 else sh_ref[b - 1, rows, lanes]
                        acc = acc + dw_ref[k:k + 1, lanes] * win
            acc_ref[r0:r0 + rb, lanes] = acc
    _ln_swish(acc_ref, z_ref, lg_ref, lb_ref)
    sc_ref[...] = ext_ref[HALO_C + tl - (CONV_C_K - 1):HALO_C + tl, :]


def _odd_prompt(u, conv_sample, dw_w, dw_b, ln_g, ln_b, *, n_batch, seq):
    dc = u.shape[1] // 2
    tl = ROW_TILE
    tiles = seq // tl
    n_tiles = n_batch * tiles
    halo_per_tile = tl // HALO_C
    assert conv_sample.shape == (tl, dc)
    vec = pl.BlockSpec((1, dc), lambda s: (0, 0))

    def tile(s):
        return jnp.minimum(s, n_tiles - 1)

    return pl.pallas_call(
        functools.partial(_odd_prompt_body, tl=tl, dc=dc, tiles=tiles, n_tiles=n_tiles),
        grid=(n_tiles + 1,),
        in_specs=[
            pl.BlockSpec((tl, 2 * dc), lambda s: (tile(s), 0)),
            pl.BlockSpec((HALO_C, 2 * dc), lambda s: (jnp.maximum(tile(s) * halo_per_tile - 1, 0), 0)),
            pl.BlockSpec((tl, dc), lambda s: (0, 0)),
            pl.BlockSpec((CONV_C_K, dc), lambda s: (0, 0)),
            vec, vec, vec,
        ],
        out_specs=[
            pl.BlockSpec((tl, dc), lambda s: (s, 0)),
            pl.BlockSpec((None, CONV_C_K - 1, dc), lambda s: (tile(s) // tiles, 0, 0)),
        ],
        out_shape=[
            jax.ShapeDtypeStruct(((n_tiles + 1) * tl, dc), bf16),
            jax.ShapeDtypeStruct((n_batch, CONV_C_K - 1, dc), f32),
        ],
        scratch_shapes=[pltpu.VMEM((HALO_C + tl, dc), f32), pltpu.VMEM((tl, dc), f32),
                        pltpu.VMEM((7, tl + HALO_C - 8, dc), f32)],
        compiler_params=_params(("arbitrary",)),
        name="odd_prompt",
    )(u, u, conv_sample, dw_w, dw_b.reshape(1, dc), ln_g.reshape(1, dc), ln_b.reshape(1, dc))


def _odd_sample_conv_body(val_ref, gate_ref, sc_ref, dw_ref, db_ref, v_ref, a_ref):
    v = val_ref[...] * _sigmoid(gate_ref[...])
    v_ref[...] = v
    acc = dw_ref[CONV_C_K - 1:CONV_C_K, :] * v + db_ref[...]
    for k in range(CONV_C_K - 1):
        acc = acc + dw_ref[k:k + 1, :] * sc_ref[k]
    a_ref[...] = acc


def _odd_sample(u, state_c_t, dw_w, dw_b, *, row_block):
    dc = u.shape[1] // 2
    nb = state_c_t.shape[1]
    cc = MM_COL_TILE
    nchunk = dc // cc
    return pl.pallas_call(
        _odd_sample_conv_body,
        grid=(nchunk,),
        in_specs=[
            pl.BlockSpec((nb, cc), lambda j: (row_block, j)),
            pl.BlockSpec((nb, cc), lambda j: (row_block, nchunk + j)),
            pl.BlockSpec((CONV_C_K - 1, nb, cc), lambda j: (0, 0, j)),
            pl.BlockSpec((CONV_C_K, cc), lambda j: (0, j)),
            pl.BlockSpec((1, cc), lambda j: (0, j)),
        ],
        out_specs=[pl.BlockSpec((nb, cc), lambda j: (0, j))] * 2,
        out_shape=[jax.ShapeDtypeStruct((nb, dc), f32)] * 2,
        compiler_params=_params(("arbitrary",)),
        name="odd_sample_conv",
    )(u, u, state_c_t, dw_w, dw_b.reshape(1, dc))


def _gather_body(src_ref, h_hbm, o_ref, buf_ref, sem):
    i, n = pl.program_id(0), pl.num_programs(0)

    def row_copy(blk, r, tok):
        slot = lax.rem(blk, 2)
        return pltpu.make_async_copy(h_hbm.at[pl.ds(tok, 1), :], buf_ref.at[slot, pl.ds(r, 1), :], sem.at[slot])

    def for_valid_rows(blk, fn):
        def body(q, carry):
            for s in range(GATHER_ROWS // GATHER_STRIDE):
                r = s * GATHER_STRIDE + q
                tok = src_ref[blk * GATHER_ROWS + r]

                @pl.when(tok >= 0)
                def _():
                    fn(row_copy(blk, r, tok), s % 2)
            return carry
        lax.fori_loop(0, GATHER_STRIDE, body, 0)

    @pl.when(i == 0)
    def _():
        buf_ref[...] = jnp.zeros(buf_ref.shape, buf_ref.dtype)
        for_valid_rows(0, lambda cp, queue: cp.start(priority=queue))

    @pl.when(i + 1 < n)
    def _():
        for_valid_rows(i + 1, lambda cp, queue: cp.start(priority=queue))

    for_valid_rows(i, lambda cp, queue: cp.wait())
    o_ref[...] = buf_ref[lax.rem(i, 2)].astype(o_ref.dtype)


def _moe_gather(h, src_tok):
    p_total = src_tok.shape[0]
    d = h.shape[1]
    return pl.pallas_call(
        _gather_body,
        grid_spec=pltpu.PrefetchScalarGridSpec(
            num_scalar_prefetch=1,
            grid=(p_total // GATHER_ROWS,),
            in_specs=[pl.BlockSpec(memory_space=pl.ANY)],
            out_specs=pl.BlockSpec((GATHER_ROWS, d), lambda i, src: (i, 0)),
            scratch_shapes=[pltpu.VMEM((2, GATHER_ROWS, d), f32), pltpu.SemaphoreType.DMA((2,))],
        ),
        out_shape=jax.ShapeDtypeStruct((p_total, d), bf16),
        compiler_params=_params(("arbitrary",)),
        name="moe_gather",
    )(src_tok, h)


def _pair_act(h, at_odd):
    n = h.shape[1]
    if at_odd:
        glu_src, lin_src = pltpu.roll(h, 1, axis=1), h
    else:
        glu_src, lin_src = h, pltpu.roll(h, n - 1, axis=1)
    glu = jnp.minimum(glu_src, SWIGLU_LIMIT)
    lin = jnp.clip(lin_src, -SWIGLU_LIMIT, SWIGLU_LIMIT)
    return glu * _sigmoid(SWIGLU_ALPHA * glu) * (lin + 1.0)


def _expert_body(se_ref, sblk_ref, snc_ref, tail_ref, x_ref, w1_ref, b1_ref, wd_ref, bd_ref, ys_hbm,
                 w1b_ref, wdb_ref, act_ref, perm_ref, obuf_ref, osem, *, j1, k2):
    del se_ref
    i, j = pl.program_id(0), pl.program_id(1)
    n_chunks = snc_ref[i]
    half = MOE_C1 // 2
    ch = MOE_CHUNK
    row0 = sblk_ref[i] * ch

    def for_row_blocks(fn):
        done = 0
        for size in MOE_BLOCKS:
            count = (n_chunks - done) // size
            rows = size * ch

            def body(b, carry, done=done, rows=rows):
                fn(pl.multiple_of(done * ch + b * rows, ch), rows)
                return carry

            lax.fori_loop(0, count, body, 0)
            done = done + count * size

    def out_copy(src_slot, src_row, rows, dst_row, col_tile):
        return pltpu.make_async_copy(
            obuf_ref.at[src_slot, pl.ds(src_row, rows), :],
            ys_hbm.at[pl.ds(pl.multiple_of(dst_row, ch), rows), pl.ds(pl.multiple_of(col_tile * MOE_C2, MOE_C2), MOE_C2)],
            osem.at[src_slot])

    def wait_tile(slot, col_tile):
        for_row_blocks(lambda r0, rows: out_copy(slot, r0, rows, row0 + r0, col_tile).wait())

    @pl.when(j < j1)
    def _gate_up():
        w1b_ref[...] = w1_ref[...].astype(bf16)

        def rows_fn(r0, rows):
            even = (lax.broadcasted_iota(i32, (rows, half), 1) & 1) == 0
            hh = jnp.dot(x_ref[pl.ds(r0, rows), :], w1b_ref[...], preferred_element_type=f32) + b1_ref[...]
            act = jnp.where(even, _pair_act(hh[:, :half], at_odd=False), _pair_act(hh[:, half:], at_odd=True))
            act_ref[j, pl.ds(r0, rows), :] = act.astype(bf16)

        for_row_blocks(rows_fn)

    @pl.when(j >= j1)
    def _down():
        k = j - j1
        slot = lax.rem(k, 2)

        @pl.when(k >= 2)
        def _():
            wait_tile(slot, k - 2)

        for jj in range(j1):
            for c in range(MOE_C2 // LANE):
                rows = slice(jj * half, (jj + 1) * half)
                cols = slice(c * LANE, (c + 1) * LANE)
                perm_ref[c, pl.ds(0, half // 2, stride=2), :] = wd_ref[jj * half:jj * half + half // 2, cols]
                perm_ref[c, pl.ds(1, half // 2, stride=2), :] = wd_ref[jj * half + half // 2:(jj + 1) * half, cols]
                wdb_ref[rows, cols] = perm_ref[c].astype(bf16)

        def rows_fn(r0, rows):
            a = jnp.concatenate([act_ref[jj, pl.ds(r0, rows), :] for jj in range(j1)], axis=1)
            obuf_ref[slot, pl.ds(r0, rows), :] = jnp.dot(a, wdb_ref[...], preferred_element_type=f32) + bd_ref[...]
            out_copy(slot, r0, rows, row0 + r0, k).start()

        for_row_blocks(rows_fn)

        @pl.when(k == k2 - 1)
        def _drain():
            wait_tile(1 - slot, k - 1)
            wait_tile(slot, k)

            @pl.when(i == pl.num_programs(0) - 1)
            def _tail():
                obuf_ref[0, 0:ch, :] = jnp.zeros((ch, MOE_C2), f32)
                tail_row0, n_tail = tail_ref[0] * ch, tail_ref[1]

                def start(t, carry):
                    for kk in range(k2):
                        out_copy(0, 0, ch, tail_row0 + t * ch, kk).start()
                    return carry

                def wait(t, carry):
                    for kk in range(k2):
                        out_copy(0, 0, ch, tail_row0 + t * ch, kk).wait()
                    return carry

                lax.fori_loop(0, n_tail, start, 0)
                lax.fori_loop(0, n_tail, wait, 0)


def _moe_experts(xs, w_gate_up, b_gate_up, w_down, b_down, layer, ys_rows, n_super, sup_e, sup_blk, sup_nc, tail):
    d = xs.shape[1]
    de = w_down.shape[2]
    j1 = (2 * de) // MOE_C1
    k2 = d // MOE_C2
    assert k2 >= 2
    s = MOE_SUPER
    el = pl.Element

    def w1_idx(i, j, se, sb, nc, tl):
        return (layer, se[i], 0, jnp.minimum(j, j1 - 1))

    def wd_idx(i, j, se, sb, nc, tl):
        return (layer, se[i], 0, jnp.maximum(j - j1, 0))

    return pl.pallas_call(
        functools.partial(_expert_body, j1=j1, k2=k2),
        grid_spec=pltpu.PrefetchScalarGridSpec(
            num_scalar_prefetch=4,
            grid=(n_super, j1 + k2),
            in_specs=[
                pl.BlockSpec((el(s), el(d)), lambda i, j, se, sb, nc, tl: (sb[i] * MOE_CHUNK, 0),
                             pipeline_mode=pl.Buffered(1)),
                pl.BlockSpec((None, None, d, MOE_C1), w1_idx),
                pl.BlockSpec((None, None, 1, MOE_C1), w1_idx),
                pl.BlockSpec((None, None, de, MOE_C2), wd_idx),
                pl.BlockSpec((None, None, 1, MOE_C2), wd_idx),
            ],
            out_specs=pl.BlockSpec(memory_space=pl.ANY),
            scratch_shapes=[
                pltpu.VMEM((d, MOE_C1), bf16),
                pltpu.VMEM((de, MOE_C2), bf16),
                pltpu.VMEM((j1, s, MOE_C1 // 2), bf16),
                pltpu.VMEM((MOE_C2 // LANE, MOE_C1 // 2, LANE), f32),
                pltpu.VMEM((2, s, MOE_C2), f32),
                pltpu.SemaphoreType.DMA((2,)),
            ],
        ),
        out_shape=jax.ShapeDtypeStruct((ys_rows, d), f32),
        compiler_params=_params(("arbitrary", "arbitrary")),
        name="moe_experts",
    )(sup_e, sup_blk, sup_nc, tail, xs, w_gate_up, b_gate_up.reshape(*b_gate_up.shape[:2], 1, -1),
      w_down, b_down.reshape(*b_down.shape[:2], 1, -1))


def _combine_body(dest_ref, ys_hbm, gate_ref, o_ref, buf_ref, sem):
    tm = o_ref.shape[0]
    i, n = pl.program_id(0), pl.num_programs(0)

    def row_copy(tile, r, k):
        slot = lax.rem(tile, 2)
        p = dest_ref[(tile * tm + r) * TOP_K + k]
        return pltpu.make_async_copy(ys_hbm.at[pl.ds(p, 1), :], buf_ref.at[slot, k, pl.ds(r, 1), :], sem.at[slot])

    def start_tile(tile):
        def body(r, carry):
            for k in range(TOP_K):
                row_copy(tile, r, k).start()
            return carry
        lax.fori_loop(0, tm, body, 0, unroll=4)

    @pl.when(i == 0)
    def _():
        start_tile(0)

    @pl.when(i + 1 < n)
    def _():
        start_tile(i + 1)

    def wait(r, carry):
        for k in range(TOP_K):
            row_copy(i, r, k).wait()
        return carry

    lax.fori_loop(0, tm, wait, 0, unroll=4)
    slot = lax.rem(i, 2)
    g = gate_ref[...]
    acc = buf_ref[slot, 0] * g[:, 0:1]
    for k in range(1, TOP_K):
        acc = acc + buf_ref[slot, k] * g[:, k:k + 1]
    o_ref[...] = acc


def _moe_combine(ys, dest, gates, n_rows):
    d = ys.shape[1]
    tm = ROW_TILE
    return pl.pallas_call(
        _combine_body,
        grid_spec=pltpu.PrefetchScalarGridSpec(
            num_scalar_prefetch=1,
            grid=(n_rows // tm,),
            in_specs=[pl.BlockSpec(memory_space=pl.ANY), pl.BlockSpec((tm, LANE), lambda i, dest: (i, 0))],
            out_specs=pl.BlockSpec((tm, d), lambda i, dest: (i, 0)),
            scratch_shapes=[pltpu.VMEM((2, TOP_K, tm, d), f32), pltpu.SemaphoreType.DMA((2,))],
        ),
        out_shape=jax.ShapeDtypeStruct((n_rows, d), f32),
        compiler_params=_params(("arbitrary",)),
        name="moe_combine",
    )(dest, ys, gates)


def _moe_plan(top_idx, n_exp):
    t = top_idx.shape[0]
    n_assign = t * TOP_K
    flat_e = top_idx.reshape(-1)
    order = jnp.argsort(flat_e).astype(i32)
    rank = jnp.argsort(order).astype(i32)
    counts = jnp.sum((flat_e[:, None] == jnp.arange(n_exp, dtype=i32)[None, :]).astype(i32), axis=0)
    padded = (counts + MOE_CHUNK - 1) // MOE_CHUNK * MOE_CHUNK
    start = jnp.cumsum(counts) - counts
    pstart = jnp.cumsum(padded) - padded
    dest = (pstart[flat_e] + rank - start[flat_e]).astype(i32)
    max_rows = -(-(n_assign + n_exp * (MOE_CHUNK - 1)) // MOE_CHUNK) * MOE_CHUNK
    p_total = -(-(max_rows + MOE_SUPER) // GATHER_ROWS) * GATHER_ROWS
    p = jnp.arange(p_total, dtype=i32)
    e_of_p = jnp.minimum(jnp.sum((p[:, None] >= (pstart + padded)[None, :]).astype(i32), axis=1), n_exp - 1)
    in_group = p - pstart[e_of_p]
    src_assign = order[jnp.clip(start[e_of_p] + in_group, 0, n_assign - 1)]
    src_tok = jnp.where(in_group < counts[e_of_p], src_assign // TOP_K, -1).astype(i32)
    n_sup_e = (padded + MOE_SUPER - 1) // MOE_SUPER
    cum = jnp.cumsum(n_sup_e)
    max_super = n_exp + max_rows // MOE_SUPER
    sid = jnp.arange(max_super, dtype=i32)
    sup_e = jnp.minimum(jnp.searchsorted(cum, sid, side="right"), n_exp - 1).astype(i32)
    within = sid - (cum[sup_e] - n_sup_e[sup_e])
    row0 = pstart[sup_e] + within * MOE_SUPER
    rows = jnp.clip(padded[sup_e] - within * MOE_SUPER, 0, MOE_SUPER)
    rows = jnp.where(sid < cum[-1], rows, 0)
    used = jnp.sum(padded)
    tail = jnp.stack([used // MOE_CHUNK, (max_rows - used) // MOE_CHUNK]).astype(i32)
    return dict(dest=dest, src_tok=src_tok, n_super=cum[-1].astype(i32), sup_e=sup_e, ys_rows=max_rows,
                sup_blk=(row0 // MOE_CHUNK).astype(i32), sup_nc=(rows // MOE_CHUNK).astype(i32), tail=tail)


def _moe(h, top_idx, gates, w_gate_up, b_gate_up, w_down, b_down, layer):
    plan = _moe_plan(top_idx[:, :TOP_K], w_down.shape[1])
    xs = _moe_gather(h, plan["src_tok"])
    ys = _moe_experts(xs, w_gate_up, b_gate_up, w_down, b_down, layer, plan["ys_rows"],
                      plan["n_super"], plan["sup_e"], plan["sup_blk"], plan["sup_nc"], plan["tail"])
    return _moe_combine(ys, plan["dest"], gates, h.shape[0])


def kernel(x_prompt, x_sample, c_prompt, c_sample, state_conv_a, state_pool_b, state_conv_c, w_ada, b_ada, norm_gains, ab_w_in, a_conv_w, b_pool_w, b_pool_scale, ab_w_out, c_pw1_w, c_pw1_b, c_dw_w, c_dw_b, c_ln_g, c_ln_b, c_pw2_w, c_pw2_b, moe_w_router, moe_b_router, moe_w_gate_up, moe_b_gate_up, moe_w_down, moe_b_down):
    n_batch, seq, d = x_prompt.shape
    n_dec, dec_seq, _ = x_sample.shape
    depth = w_ada.shape[0]
    assert dec_seq == 1 and n_dec == ROW_TILE and seq % ROW_TILE == 0
    n_prompt_rows = n_batch * seq
    r = n_prompt_rows + n_dec
    assert r % MM_ROW_TILE == 0
    sample_block = n_prompt_rows // ROW_TILE

    x = (x_prompt.reshape(n_prompt_rows, d), x_sample.reshape(n_dec, d))
    pad = (-(n_dec + n_batch)) % 8
    c_all = jnp.concatenate([c_sample, c_prompt, jnp.zeros((pad, d), f32)], axis=0)
    mod_all = _ada(c_all, w_ada, b_ada)
    rowwise = functools.partial(_rowwise, n_prompt_rows=n_prompt_rows, rows_per_seq=seq)

    new_a_p, new_b_p, new_c_p, new_a_s, new_b_s, new_c_s = [], [], [], [], [], []
    h = None
    for l in range(depth):
        mod_s = mod_all[l, :n_dec]
        mod_p = mod_all[l, n_dec:n_dec + n_batch].reshape(n_batch, 1, N_MOD * d)
        ng = norm_gains[l]
        if l == 0:
            h = rowwise(x, mod_s, mod_p, norm=(ng[0], 1, 0))[0]
        if l % 2 == 0:
            e = l // 2
            proj = _mm(h, ab_w_in[e])
            y_s, g_s = _even_sample(proj, state_conv_a[e].transpose(1, 0, 2), state_pool_b[e].transpose(1, 0, 2),
                                    a_conv_w[e], b_pool_w[e], b_pool_scale[e], row_block=sample_block)
            ycat, sa, sb = _even_prompt(proj, y_s, a_conv_w[e], b_pool_w[e], b_pool_scale[e],
                                        n_batch=n_batch, seq=seq)
            y = _mm(ycat, ab_w_out[e])
            new_a_p.append(sa)
            new_b_p.append(sb)
            new_a_s.append(jnp.concatenate([state_conv_a[e][:, 1:], g_s[:, None]], axis=1))
            b_in_s = proj[n_prompt_rows:, 3 * (d // 2):]
            new_b_s.append(jnp.concatenate([state_pool_b[e][:, 1:], b_in_s[:, None]], axis=1))
        else:
            o = l // 2
            u = _mm(h, c_pw1_w[o], c_pw1_b[o])
            v_s, conv_s = _odd_sample(u, state_conv_c[o].transpose(1, 0, 2), c_dw_w[o], c_dw_b[o],
                                      row_block=sample_block)
            zcat, s_c = _odd_prompt(u, conv_s, c_dw_w[o], c_dw_b[o], c_ln_g[o], c_ln_b[o],
                                    n_batch=n_batch, seq=seq)
            y = _mm(zcat, c_pw2_w[o], c_pw2_b[o])
            new_c_p.append(s_c)
            new_c_s.append(jnp.concatenate([state_conv_c[o][:, 1:], v_s[:, None]], axis=1))
        x, h2, top_idx, gates = rowwise(x, mod_s, mod_p, resid=(y, ng[1], 2), norm=(ng[2], 4, 3),
                                        router=(moe_w_router[l], moe_b_router[l]), h_dtype=f32)
        y = _moe(h2, top_idx, gates, moe_w_gate_up, moe_b_gate_up, moe_w_down, moe_b_down, l)
        if l + 1 < depth:
            mod_s_n = mod_all[l + 1, :n_dec]
            mod_p_n = mod_all[l + 1, n_dec:n_dec + n_batch].reshape(n_batch, 1, N_MOD * d)
            x, h = rowwise(x, mod_s, mod_p, resid=(y, ng[3], 5), norm=(norm_gains[l + 1][0], 1, 0),
                           norm_mods=(mod_s_n, mod_p_n))
        else:
            x = rowwise(x, mod_s, mod_p, resid=(y, ng[3], 5), split_out=True)

    y_prompt = x[0].reshape(n_batch, seq, d)
    y_sample = x[1].reshape(n_dec, 1, d)
    return (y_prompt, y_sample, jnp.stack(new_a_p), jnp.stack(new_b_p), jnp.stack(new_c_p),
            jnp.stack(new_a_s), jnp.stack(new_b_s), jnp.stack(new_c_s))
```
